```python
import math
import jax, jax.numpy as jnp
from jax import lax
import numpy as np

D_MODEL = 1024
BATCH = 8
SEQ = 4096
DEPTH = 4

GRID_W = 64
NA_HEADS = 8
NA_HEAD_DIM = 64
NA_WIN_ROWS = 8
NA_WIN_COLS = 16
MLA_HEADS = 8
MLA_Q_RANK = 256
MLA_KV_RANK = 128
MLA_NOPE_DIM = 64
MLA_ROPE_DIM = 32
MLA_V_DIM = 64
ROPE_THETA = 10000.0
D_FF = 4 * D_MODEL
Q_BLOCK = 128
NORM_EPS = 1e-6
NA_WIDTH = NA_HEADS * NA_HEAD_DIM
MLA_WIDTH = MLA_HEADS * MLA_V_DIM
IN_SIZES = (NA_WIDTH, NA_WIDTH, NA_WIDTH, MLA_Q_RANK, MLA_KV_RANK, MLA_ROPE_DIM, 2 * D_MODEL)
IN_COLS = sum(IN_SIZES)
RPB_ROWS = 2 * NA_WIN_ROWS - 1
RPB_COLS = 2 * NA_WIN_COLS - 1

kernel_name = "hybrid_natten_mla_sqrelu_adaln_encoder"


def rmsnorm(x, g):
    xf = x.astype(jnp.float32)
    y = xf * lax.rsqrt(jnp.mean(xf * xf, axis=-1, keepdims=True) + NORM_EPS)
    return (y * g.astype(jnp.float32)).astype(x.dtype)


def apply_rope(x, cos, sin):
    half = x.shape[-1] // 2
    x1, x2 = x[..., :half], x[..., half:]
    return jnp.concatenate([x1 * cos - x2 * sin, x2 * cos + x1 * sin], axis=-1)


def rope_tables(positions, dtype):
    inv_freq = 1.0 / (ROPE_THETA ** (jnp.arange(0, MLA_ROPE_DIM, 2, dtype=jnp.float32) / MLA_ROPE_DIM))
    ang = positions.astype(jnp.float32)[..., None] * inv_freq
    return jnp.cos(ang).astype(dtype), jnp.sin(ang).astype(dtype)


def split_cols(t, sizes):
    idx = []
    acc = 0
    for s in sizes[:-1]:
        acc += s
        idx.append(acc)
    return jnp.split(t, idx, axis=-1)


def neighborhood_attention(q, k, v, rpb):
    B, S, H, dh = q.shape
    rows = S // GRID_W
    kh = min(NA_WIN_ROWS, rows)
    kw = NA_WIN_COLS

    def to_grid(t):
        return t.reshape(B, rows, GRID_W, H, dh).transpose(1, 0, 3, 2, 4)

    qg, kg, vg = to_grid(q), to_grid(k), to_grid(v)
    col = jnp.arange(GRID_W)
    col_start = jnp.clip(col - kw // 2, 0, GRID_W - kw)
    col_idx = col_start[:, None] + jnp.arange(kw)
    col_off = col_idx - col[:, None] + (NA_WIN_COLS - 1)
    row = jnp.arange(rows)
    row_start = jnp.clip(row - kh // 2, 0, rows - kh)
    scale = dh ** -0.5

    def one_row(args):
        r, rs, q_r = args
        k_blk = lax.dynamic_slice_in_dim(kg, rs, kh, axis=0)
        v_blk = lax.dynamic_slice_in_dim(vg, rs, kh, axis=0)
        k_nb = k_blk[:, :, :, col_idx]
        v_nb = v_blk[:, :, :, col_idx]
        s = jnp.einsum('bhcd,ibhcjd->bhcij', q_r, k_nb).astype(jnp.float32) * scale
        row_off = rs + jnp.arange(kh) - r + (NA_WIN_ROWS - 1)
        bias = rpb[:, row_off[:, None, None], col_off[None, :, :]]
        s = s + bias.transpose(0, 2, 1, 3)[None].astype(jnp.float32)
        p = jax.nn.softmax(s.reshape(B, H, GRID_W, kh * kw), axis=-1)
        p = p.reshape(B, H, GRID_W, kh, kw).astype(v.dtype)
        return jnp.einsum('bhcij,ibhcjd->bhcd', p, v_nb)

    out = lax.map(one_row, (row, row_start, qg))
    return out.transpose(1, 0, 3, 2, 4).reshape(B, S, H * dh)


def mla_attention(c_q, c_kv, k_rope, w_uq, w_ukv, g_q, g_kv, cos, sin):
    B, S, _ = c_q.shape
    H = MLA_HEADS
    q = (rmsnorm(c_q, g_q) @ w_uq).reshape(B, S, H, MLA_NOPE_DIM + MLA_ROPE_DIM)
    q_nope, q_rope = q[..., :MLA_NOPE_DIM], q[..., MLA_NOPE_DIM:]
    q_rope = apply_rope(q_rope, cos[:, :, None], sin[:, :, None])
    kv = (rmsnorm(c_kv, g_kv) @ w_ukv).reshape(B, S, H, MLA_NOPE_DIM + MLA_V_DIM)
    k_nope, v = kv[..., :MLA_NOPE_DIM], kv[..., MLA_NOPE_DIM:]
    k_r = apply_rope(k_rope, cos, sin)
    nblk = S // Q_BLOCK
    scale = (MLA_NOPE_DIM + MLA_ROPE_DIM) ** -0.5
    qn = q_nope.reshape(B, nblk, Q_BLOCK, H, MLA_NOPE_DIM).transpose(1, 0, 3, 2, 4)
    qr = q_rope.reshape(B, nblk, Q_BLOCK, H, MLA_ROPE_DIM).transpose(1, 0, 3, 2, 4)

    def one_block(args):
        qn_b, qr_b = args
        s = (jnp.einsum('bhqd,bkhd->bhqk', qn_b, k_nope)
             + jnp.einsum('bhqr,bkr->bhqk', qr_b, k_r))
        p = jax.nn.softmax(s.astype(jnp.float32) * scale, axis=-1).astype(v.dtype)
        return jnp.einsum('bhqk,bkhd->bqhd', p, v)

    out = lax.map(one_block, (qn, qr))
    return out.transpose(1, 0, 2, 3, 4).reshape(B, S, H * MLA_V_DIM)


def setup_inputs(seed: int = 0) -> dict:
    key = jax.random.key(seed)
    ks = jax.random.split(key, 24)
    L, D = DEPTH, D_MODEL

    def w(k, shape, fan_in):
        return jax.random.normal(k, shape, jnp.float32) * (fan_in ** -0.5)

    def gain(k, shape):
        return 1.0 + 0.05 * jax.random.normal(k, shape, jnp.float32)

    x = jax.random.normal(ks[0], (BATCH, SEQ, D), jnp.float32)
    c = jax.random.normal(ks[1], (BATCH, D), jnp.float32)
    offset = jax.random.randint(ks[2], (BATCH, 1), 0, 1024, dtype=jnp.int32)
    positions = offset + jnp.arange(SEQ, dtype=jnp.int32)[None, :]
    return {
        "x": x,
        "c": c,
        "positions": positions,
        "w_ada": w(ks[3], (L, D, 6 * D), D),
        "b_ada": 0.02 * jax.random.normal(ks[4], (L, 6 * D), jnp.float32),
        "g_mix": gain(ks[5], (L, D)),
        "w_in": w(ks[6], (L, D, IN_COLS), D),
        "b_gate": 0.02 * jax.random.normal(ks[7], (L, 2 * D), jnp.float32),
        "rpb": 0.05 * jax.random.normal(ks[8], (L, NA_HEADS, RPB_ROWS, RPB_COLS), jnp.float32),
        "g_q": gain(ks[9], (L, MLA_Q_RANK)),
        "w_uq": w(ks[10], (L, MLA_Q_RANK, MLA_HEADS * (MLA_NOPE_DIM + MLA_ROPE_DIM)), MLA_Q_RANK),
        "g_kv": gain(ks[11], (L, MLA_KV_RANK)),
        "w_ukv": w(ks[12], (L, MLA_KV_RANK, MLA_HEADS * (MLA_NOPE_DIM + MLA_V_DIM)), MLA_KV_RANK),
        "w_br_na": w(ks[13], (L, NA_WIDTH, D), NA_WIDTH),
        "w_br_mla": w(ks[14], (L, MLA_WIDTH, D), MLA_WIDTH),
        "w_out": w(ks[15], (L, D, D), D),
        "g_mlp": gain(ks[16], (L, D)),
        "w_ff1": w(ks[17], (L, D, D_FF), D),
        "w_ff2": w(ks[18], (L, D_FF, D), D_FF),
        "g_final": gain(ks[19], (D,)),
    }


def reference(x, c, positions, w_ada, b_ada, g_mix, w_in, b_gate, rpb, g_q, w_uq, g_kv, w_ukv,
              w_br_na, w_br_mla, w_out, g_mlp, w_ff1, w_ff2, g_final):
    B, S, D = x.shape
    cos, sin = rope_tables(positions, x.dtype)
    c_act = jax.nn.silu(c)
    for l in range(DEPTH):
        mod = c_act @ w_ada[l] + b_ada[l]
        shift_a, scale_a, gate_a, shift_m, scale_m, gate_m = jnp.split(mod, 6, axis=-1)

        h = rmsnorm(x, g_mix[l]) * (1.0 + scale_a[:, None]) + shift_a[:, None]
        proj = h @ w_in[l]
        q_na, k_na, v_na, c_q, c_kv, k_rope, gate_logits = split_cols(proj, IN_SIZES)
        hd = (B, S, NA_HEADS, NA_HEAD_DIM)
        o_na = neighborhood_attention(q_na.reshape(hd), k_na.reshape(hd), v_na.reshape(hd), rpb[l])
        o_mla = mla_attention(c_q, c_kv, k_rope, w_uq[l], w_ukv[l], g_q[l], g_kv[l], cos, sin)
        gates = jax.nn.sigmoid((gate_logits + b_gate[l]).astype(jnp.float32)).astype(x.dtype)
        g_na, g_mla = gates[..., :D], gates[..., D:]
        merged = g_na * (o_na @ w_br_na[l]) + g_mla * (o_mla @ w_br_mla[l])
        x = x + gate_a[:, None] * (merged @ w_out[l])

        h = rmsnorm(x, g_mlp[l]) * (1.0 + scale_m[:, None]) + shift_m[:, None]
        x = x + gate_m[:, None] * (jnp.square(jax.nn.relu(h @ w_ff1[l])) @ w_ff2[l])
    return rmsnorm(x, g_final)
```

```python
import functools

import jax
import jax.numpy as jnp
from jax import lax
from jax.experimental import pallas as pl
from jax.experimental.pallas import tpu as pltpu

F32 = jnp.float32
BF16 = jnp.bfloat16

GRID_W = 64
NA_HEADS = 8
NA_HEAD_DIM = 64
NA_WIN_ROWS = 8
NA_WIN_COLS = 16
MLA_HEADS = 8
MLA_Q_RANK = 256
MLA_KV_RANK = 128
MLA_NOPE_DIM = 64
MLA_ROPE_DIM = 32
MLA_V_DIM = 64
ROPE_THETA = 10000.0
NORM_EPS = 1e-6
NA_WIDTH = NA_HEADS * NA_HEAD_DIM
MLA_WIDTH = MLA_HEADS * MLA_V_DIM

LANES = 128
HEAD_PAD = 128
VMEM_LIMIT_BYTES = 56 * 1024 * 1024

TOKEN_TILE = 512
POST_TILE = 256
MLA_Q_TILE = 256
NA_Q_ROWS = 4
NA_KEY_ROWS = 12
NA_Q_TILE = NA_Q_ROWS * GRID_W
NA_KEY_TILE = NA_KEY_ROWS * GRID_W
MASK_VALUE = -1e30


def _const_spec(shape, index_map):
    return pl.BlockSpec(shape, index_map, pipeline_mode=pl.Buffered(1))


def _rms(x, gain):
    return x * lax.rsqrt(jnp.mean(x * x, axis=-1, keepdims=True) + NORM_EPS) * gain


def _nt_dot(a, b):
    return lax.dot_general(a, b, (((1,), (1,)), ((), ())), preferred_element_type=F32)


def _ada_kernel(c_ref, w_ref, b_ref, o_ref):
    c = c_ref[...]
    c_act = (c / (1.0 + jnp.exp(-c))).astype(BF16)
    o_ref[0] = jnp.dot(c_act, w_ref[0].astype(BF16), preferred_element_type=F32) + b_ref[0]


def _ada_mod(c, w_ada, b_ada):
    depth, d, d6 = w_ada.shape
    batch = c.shape[0]
    nchunk = d6 // d
    return pl.pallas_call(
        _ada_kernel,
        grid=(depth, nchunk),
        in_specs=[
            pl.BlockSpec((batch, d), lambda l, j: (0, 0)),
            pl.BlockSpec((1, d, d), lambda l, j: (l, 0, j)),
            pl.BlockSpec((1, 1, d), lambda l, j: (l, 0, j)),
        ],
        out_specs=pl.BlockSpec((1, batch, d), lambda l, j: (l, 0, j)),
        out_shape=jax.ShapeDtypeStruct((depth, batch, d6), F32),
        name="ada_mod",
    )(c, w_ada, b_ada.reshape(depth, 1, d6))


def _proj_kernel(x_ref, mod_ref, gmix_ref, wqv_ref, wk_ref, wc_ref, wg_ref, bg_ref, gq_ref, gkv_ref,
                 wuq_ref, wkk_ref, wuv_ref, cost_ref, sint_ref, cos2_ref, sin2_ref,
                 qna_ref, kna_ref, vna_ref, qm_ref, km_ref, vm_ref, gates_ref):
    tm = x_ref.shape[1]
    x = x_ref[0]
    mod = mod_ref[0]
    shift, scale = mod[0:1], mod[1:2]
    h = (_rms(x, gmix_ref[...]) * (1.0 + scale) + shift).astype(BF16)

    qv_t = _nt_dot(wqv_ref[...], h)
    q_t = (qv_t[:NA_WIDTH] * (NA_HEAD_DIM ** -0.5)).astype(BF16)
    zeros = jnp.zeros((NA_HEAD_DIM, tm), BF16)
    for hd in range(NA_HEADS):
        q_h = q_t[hd * NA_HEAD_DIM:(hd + 1) * NA_HEAD_DIM]
        lo, hi = (q_h, zeros) if hd % 2 == 0 else (zeros, q_h)
        qna_ref[0, hd, :NA_HEAD_DIM, :] = lo
        qna_ref[0, hd, NA_HEAD_DIM:, :] = hi
    v_t = qv_t[NA_WIDTH:].astype(BF16)
    for ch in range(tm // NA_Q_TILE):
        vna_ref[0, ch] = v_t[:, ch * NA_Q_TILE:(ch + 1) * NA_Q_TILE]
    kna_ref[0] = jnp.dot(h, wk_ref[...], preferred_element_type=F32).astype(BF16)

    logits = jnp.dot(h, wg_ref[...], preferred_element_type=F32) + bg_ref[...]
    gates_ref[0] = (1.0 / (1.0 + jnp.exp(-logits))).astype(BF16)

    lat = jnp.dot(h, wc_ref[...], preferred_element_type=F32)
    c_q = lat[:, :MLA_Q_RANK]
    c_kv = lat[:, MLA_Q_RANK:MLA_Q_RANK + MLA_KV_RANK]
    kr = lat[:, MLA_Q_RANK + MLA_KV_RANK:MLA_Q_RANK + MLA_KV_RANK + LANES]
    kr_swapped = lat[:, MLA_Q_RANK + MLA_KV_RANK + LANES:]
    cq_n = _rms(c_q, gq_ref[...]).astype(BF16)
    ckv_n = _rms(c_kv, gkv_ref[...]).astype(BF16)

    q3 = _nt_dot(wuq_ref[...], cq_n).reshape(MLA_HEADS, HEAD_PAD, tm)
    cos_t, sin_t = cost_ref[0], sint_ref[0]
    half = MLA_ROPE_DIM // 2
    x1 = q3[:, MLA_NOPE_DIM:MLA_NOPE_DIM + half]
    x2 = q3[:, MLA_NOPE_DIM + half:MLA_NOPE_DIM + MLA_ROPE_DIM]
    q3 = jnp.concatenate(
        [q3[:, :MLA_NOPE_DIM], x1 * cos_t - x2 * sin_t, x2 * cos_t + x1 * sin_t,
         q3[:, MLA_NOPE_DIM + MLA_ROPE_DIM:]], axis=1)
    qm_ref[0] = (q3 * ((MLA_NOPE_DIM + MLA_ROPE_DIM) ** -0.5)).astype(BF16)

    k_rot = (kr * cos2_ref[0] + kr_swapped * sin2_ref[0]).astype(BF16)
    k_all = jnp.dot(jnp.concatenate([ckv_n, k_rot], axis=1), wkk_ref[...],
                    preferred_element_type=F32).astype(BF16)
    for hd in range(MLA_HEADS):
        km_ref[0, hd] = k_all[:, hd * HEAD_PAD:(hd + 1) * HEAD_PAD]
    vm_ref[0, :, 0] = _nt_dot(wuv_ref[...], ckv_n).astype(BF16).reshape(MLA_HEADS, MLA_V_DIM, tm)


def _projections(x, mod_l, layer, p, rope):
    batch, seq, d = x.shape
    tm = TOKEN_TILE
    nt = seq // tm
    cos_t, sin_t, cos2, sin2 = rope
    lat_cols = p["wc"].shape[-1]
    tok = lambda b, t: (b, t, 0)
    per_batch = lambda b, t: (b, 0, 0)
    w3 = lambda b, t: (layer, 0, 0)
    out_shapes = (
        jax.ShapeDtypeStruct((batch, NA_HEADS, HEAD_PAD, seq), BF16),
        jax.ShapeDtypeStruct((batch, seq, NA_WIDTH), BF16),
        jax.ShapeDtypeStruct((batch, seq // NA_Q_TILE, NA_WIDTH, NA_Q_TILE), BF16),
        jax.ShapeDtypeStruct((batch, MLA_HEADS, HEAD_PAD, seq), BF16),
        jax.ShapeDtypeStruct((batch, MLA_HEADS, seq, HEAD_PAD), BF16),
        jax.ShapeDtypeStruct((batch, MLA_HEADS, nt, MLA_V_DIM, tm), BF16),
        jax.ShapeDtypeStruct((batch, seq, 2 * d), BF16),
    )
    out_specs = (
        pl.BlockSpec((1, NA_HEADS, HEAD_PAD, tm), lambda b, t: (b, 0, 0, t)),
        pl.BlockSpec((1, tm, NA_WIDTH), tok),
        pl.BlockSpec((1, tm // NA_Q_TILE, NA_WIDTH, NA_Q_TILE), lambda b, t: (b, t, 0, 0)),
        pl.BlockSpec((1, MLA_HEADS, HEAD_PAD, tm), lambda b, t: (b, 0, 0, t)),
        pl.BlockSpec((1, MLA_HEADS, tm, HEAD_PAD), lambda b, t: (b, 0, t, 0)),
        pl.BlockSpec((1, MLA_HEADS, 1, MLA_V_DIM, tm), lambda b, t: (b, 0, t, 0, 0)),
        pl.BlockSpec((1, tm, 2 * d), tok),
    )
    in_specs = [
        pl.BlockSpec((1, tm, d), tok),
        pl.BlockSpec((1, 6, d), per_batch),
        _const_spec((None, 1, d), w3),
        _const_spec((None, 2 * NA_WIDTH, d), w3),
        _const_spec((None, d, NA_WIDTH), w3),
        _const_spec((None, d, lat_cols), w3),
        _const_spec((None, d, 2 * d), w3),
        _const_spec((None, 1, 2 * d), w3),
        _const_spec((None, 1, MLA_Q_RANK), w3),
        _const_spec((None, 1, MLA_KV_RANK), w3),
        _const_spec((None, MLA_HEADS * HEAD_PAD, MLA_Q_RANK), w3),
        _const_spec((None, MLA_KV_RANK + LANES, MLA_HEADS * HEAD_PAD), w3),
        _const_spec((None, MLA_WIDTH, MLA_KV_RANK), w3),
        pl.BlockSpec((1, MLA_ROPE_DIM // 2, tm), lambda b, t: (b, 0, t)),
        pl.BlockSpec((1, MLA_ROPE_DIM // 2, tm), lambda b, t: (b, 0, t)),
        pl.BlockSpec((1, tm, LANES), tok),
        pl.BlockSpec((1, tm, LANES), tok),
    ]
    return pl.pallas_call(
        _proj_kernel,
        grid=(batch, nt),
        in_specs=in_specs,
        out_specs=out_specs,
        out_shape=out_shapes,
        compiler_params=pltpu.CompilerParams(
            dimension_semantics=("parallel", "parallel"), vmem_limit_bytes=VMEM_LIMIT_BYTES),
        name="projections",
    )(x, mod_l, p["g_mix"], p["wqv_t"], p["wk"], p["wc"], p["wg"], p["b_gate"], p["g_q"], p["g_kv"],
      p["wuq_t"], p["wkk"], p["wuv_t"], cos_t, sin_t, cos2, sin2)


def _na_kernel(q_ref, k_ref, v_ref, bias_ref, o_ref, ot_ref, *, grid_rows):
    i = pl.program_id(1)
    nblk = pl.num_programs(1)
    key_row0 = jnp.clip(NA_Q_ROWS * i - NA_WIN_ROWS // 2, 0, grid_rows - NA_KEY_ROWS)
    key0 = pl.multiple_of(key_row0 * GRID_W, NA_Q_TILE)
    chunk0 = key_row0 // NA_Q_ROWS
    pattern = jnp.where(i == 0, 0, jnp.where(i == nblk - 1, 2, 1))
    nchunk = NA_KEY_TILE // NA_Q_TILE
    for hd in range(NA_HEADS):
        pair = hd // 2
        k_pair = k_ref[0, pl.ds(key0, NA_KEY_TILE), pair * LANES:(pair + 1) * LANES]
        s = jnp.dot(k_pair, q_ref[0, hd], preferred_element_type=F32)
        s = s + bias_ref[pattern, hd]
        m = jnp.max(s, axis=0, keepdims=True)
        p = jnp.exp(s - m)
        denom = jnp.sum(p, axis=0, keepdims=True)
        p = p.astype(BF16)
        o_t = jnp.zeros((NA_HEAD_DIM, NA_Q_TILE), F32)
        for ch in range(nchunk):
            v_t = v_ref[0, chunk0 + ch, hd * NA_HEAD_DIM:(hd + 1) * NA_HEAD_DIM, :]
            o_t = o_t + jnp.dot(v_t, p[ch * NA_Q_TILE:(ch + 1) * NA_Q_TILE],
                                preferred_element_type=F32)
        ot_ref[hd * NA_HEAD_DIM:(hd + 1) * NA_HEAD_DIM, :] = o_t * (1.0 / denom)
    o_ref[0] = ot_ref[...].T.astype(BF16)


def _na_attention(q_t, k, v_t, bias, layer):
    batch, seq, _ = k.shape
    grid_rows = seq // GRID_W
    nblk = grid_rows // NA_Q_ROWS
    return pl.pallas_call(
        functools.partial(_na_kernel, grid_rows=grid_rows),
        grid=(batch, nblk),
        in_specs=[
            pl.BlockSpec((1, NA_HEADS, HEAD_PAD, NA_Q_TILE), lambda b, i: (b, 0, 0, i)),
            pl.BlockSpec((1, seq, NA_WIDTH), lambda b, i: (b, 0, 0)),
            pl.BlockSpec((1, seq // NA_Q_TILE, NA_WIDTH, NA_Q_TILE), lambda b, i: (b, 0, 0, 0)),
            _const_spec((None, 3, NA_HEADS, NA_KEY_TILE, NA_Q_TILE), lambda b, i: (layer, 0, 0, 0, 0)),
        ],
        out_specs=pl.BlockSpec((1, NA_Q_TILE, NA_WIDTH), lambda b, i: (b, i, 0)),
        out_shape=jax.ShapeDtypeStruct((batch, seq, NA_WIDTH), BF16),
        scratch_shapes=[pltpu.VMEM((NA_WIDTH, NA_Q_TILE), F32)],
        compiler_params=pltpu.CompilerParams(
            dimension_semantics=("parallel", "arbitrary"), vmem_limit_bytes=VMEM_LIMIT_BYTES),
        name="na_attention",
    )(q_t, k, v_t, bias)


def _na_bias_tables(rpb, grid_rows):
    nblk = grid_rows // NA_Q_ROWS
    kc = jnp.arange(GRID_W)[:, None]
    c = jnp.arange(GRID_W)[None, :]
    cs = jnp.clip(c - NA_WIN_COLS // 2, 0, GRID_W - NA_WIN_COLS)
    col_ok = (kc >= cs) & (kc < cs + NA_WIN_COLS)
    col_off = jnp.clip(kc - c + (NA_WIN_COLS - 1), 0, 2 * NA_WIN_COLS - 2)
    tables = []
    for i in (0, 1, nblk - 1):
        key_row0 = min(max(NA_Q_ROWS * i - NA_WIN_ROWS // 2, 0), grid_rows - NA_KEY_ROWS)
        r = (NA_Q_ROWS * i + jnp.arange(NA_Q_ROWS))[None, :]
        rs = jnp.clip(r - NA_WIN_ROWS // 2, 0, grid_rows - NA_WIN_ROWS)
        kr = (key_row0 + jnp.arange(NA_KEY_ROWS))[:, None]
        row_ok = (kr >= rs) & (kr < rs + NA_WIN_ROWS)
        row_off = jnp.clip(kr - r + (NA_WIN_ROWS - 1), 0, 2 * NA_WIN_ROWS - 2)
        b = rpb[:, :, row_off[:, None, :, None], col_off[None, :, None, :]]
        ok = row_ok[:, None, :, None] & col_ok[None, :, None, :]
        b = jnp.where(ok, b, MASK_VALUE)
        tables.append(b.reshape(rpb.shape[0], NA_HEADS, NA_KEY_TILE, NA_Q_TILE))
    return jnp.stack(tables, axis=1)


def _mla_kernel(q_ref, k_ref, v_ref, o_ref, ot_ref):
    tq = q_ref.shape[-1]
    nk = v_ref.shape[2]
    tk = v_ref.shape[-1]

    def head_body(hd, carry):
        q_t = q_ref[0, hd]

        def key_body(j, state):
            m, denom, acc = state
            k0 = pl.multiple_of(j * tk, tk)
            s = jnp.dot(k_ref[0, hd, pl.ds(k0, tk), :], q_t, preferred_element_type=F32)
            m_new = jnp.maximum(m, jnp.max(s, axis=0, keepdims=True))
            alpha = jnp.exp(m - m_new)
            p = jnp.exp(s - m_new)
            denom = alpha * denom + jnp.sum(p, axis=0, keepdims=True)
            acc = alpha * acc + jnp.dot(v_ref[0, hd, j], p.astype(BF16), preferred_element_type=F32)
            return m_new, denom, acc

        init = (jnp.full((1, tq), MASK_VALUE, F32), jnp.zeros((1, tq), F32),
                jnp.zeros((MLA_V_DIM, tq), F32))
        _, denom, acc = lax.fori_loop(0, nk, key_body, init)
        row0 = pl.multiple_of(hd * MLA_V_DIM, MLA_V_DIM)
        ot_ref[pl.ds(row0, MLA_V_DIM), :] = acc * (1.0 / denom)
        return carry

    lax.fori_loop(0, MLA_HEADS, head_body, 0)
    o_ref[0] = ot_ref[...].T.astype(BF16)


def _mla_attention(q_t, k, v_t):
    batch, _, seq, _ = k.shape
    nk, tk = v_t.shape[2], v_t.shape[4]
    tq = MLA_Q_TILE
    return pl.pallas_call(
        _mla_kernel,
        grid=(batch, seq // tq),
        in_specs=[
            pl.BlockSpec((1, MLA_HEADS, HEAD_PAD, tq), lambda b, i: (b, 0, 0, i)),
            pl.BlockSpec((1, MLA_HEADS, seq, HEAD_PAD), lambda b, i: (b, 0, 0, 0)),
            pl.BlockSpec((1, MLA_HEADS, nk, MLA_V_DIM, tk), lambda b, i: (b, 0, 0, 0, 0)),
        ],
        out_specs=pl.BlockSpec((1, tq, MLA_WIDTH), lambda b, i: (b, i, 0)),
        out_shape=jax.ShapeDtypeStruct((batch, seq, MLA_WIDTH), BF16),
        scratch_shapes=[pltpu.VMEM((MLA_WIDTH, tq), F32)],
        compiler_params=pltpu.CompilerParams(
            dimension_semantics=("parallel", "arbitrary"), vmem_limit_bytes=VMEM_LIMIT_BYTES),
        name="mla_attention",
    )(q_t, k, v_t)


def _post_kernel(x_ref, ona_ref, omla_ref, gates_ref, mod_ref, wbn_ref, wbm_ref, wout_ref, gmlp_ref,
                 w1_ref, w2_ref, gfin_ref, o_ref, *, final):
    d = x_ref.shape[-1]
    x = x_ref[0]
    mod = mod_ref[0]
    gate_a, shift_m, scale_m, gate_m = mod[2:3], mod[3:4], mod[4:5], mod[5:6]
    gates = gates_ref[0]
    br_na = jnp.dot(ona_ref[0], wbn_ref[...], preferred_element_type=F32)
    br_mla = jnp.dot(omla_ref[0], wbm_ref[...], preferred_element_type=F32)
    merged = gates[:, :d].astype(F32) * br_na + gates[:, d:].astype(F32) * br_mla
    x = x + gate_a * jnp.dot(merged.astype(BF16), wout_ref[...], preferred_element_type=F32)

    h = (_rms(x, gmlp_ref[...]) * (1.0 + scale_m) + shift_m).astype(BF16)
    u = jnp.dot(h, w1_ref[...], preferred_element_type=F32)
    u = jnp.square(jnp.maximum(u, 0.0)).astype(BF16)
    x = x + gate_m * jnp.dot(u, w2_ref[...], preferred_element_type=F32)
    if final:
        x = _rms(x, gfin_ref[...])
    o_ref[0] = x


def _post(x, o_na, o_mla, gates, mod_l, layer, p, g_final, final):
    batch, seq, d = x.shape
    d_ff = p["w1"].shape[-1]
    tm = POST_TILE
    tok = lambda b, t: (b, t, 0)
    w3 = lambda b, t: (layer, 0, 0)
    return pl.pallas_call(
        functools.partial(_post_kernel, final=final),
        grid=(batch, seq // tm),
        in_specs=[
            pl.BlockSpec((1, tm, d), tok),
            pl.BlockSpec((1, tm, NA_WIDTH), tok),
            pl.BlockSpec((1, tm, MLA_WIDTH), tok),
            pl.BlockSpec((1, tm, 2 * d), tok),
            pl.BlockSpec((1, 6, d), lambda b, t: (b, 0, 0)),
            _const_spec((None, NA_WIDTH, d), w3),
            _const_spec((None, MLA_WIDTH, d), w3),
            _const_spec((None, d, d), w3),
            _const_spec((None, 1, d), w3),
            _const_spec((None, d, d_ff), w3),
            _const_spec((None, d_ff, d), w3),
            _const_spec((1, d), lambda b, t: (0, 0)),
        ],
        out_specs=pl.BlockSpec((1, tm, d), tok),
        out_shape=jax.ShapeDtypeStruct((batch, seq, d), F32),
        compiler_params=pltpu.CompilerParams(
            dimension_semantics=("parallel", "parallel"), vmem_limit_bytes=VMEM_LIMIT_BYTES),
        name="post",
    )(x, o_na, o_mla, gates, mod_l, p["w_br_na"], p["w_br_mla"], p["w_out"], p["g_mlp"], p["w1"], p["w2"],
      g_final)


def _prepare_params(g_mix, w_in, b_gate, g_q, w_uq, g_kv, w_ukv, w_br_na, w_br_mla, w_out, g_mlp, w_ff1, w_ff2):
    depth, d, _ = w_in.shape
    o = 0
    cols = {}
    for name, size in (("q_na", NA_WIDTH), ("k_na", NA_WIDTH), ("v_na", NA_WIDTH), ("c_q", MLA_Q_RANK),
                       ("c_kv", MLA_KV_RANK), ("k_rope", MLA_ROPE_DIM), ("gate", 2 * d)):
        cols[name] = w_in[:, :, o:o + size]
        o += size
    half = MLA_ROPE_DIM // 2
    rope_pad = jnp.zeros((depth, d, LANES - MLA_ROPE_DIM), F32)
    kr = cols["k_rope"]
    kr_swapped = jnp.concatenate([kr[:, :, half:], kr[:, :, :half]], axis=-1)
    wc = jnp.concatenate([cols["c_q"], cols["c_kv"], kr, rope_pad, kr_swapped, rope_pad], axis=-1)

    qk_dim = MLA_NOPE_DIM + MLA_ROPE_DIM
    wuq = w_uq.reshape(depth, MLA_Q_RANK, MLA_HEADS, qk_dim)
    wuq = jnp.pad(wuq, ((0, 0), (0, 0), (0, 0), (0, HEAD_PAD - qk_dim)))
    wuq_t = wuq.reshape(depth, MLA_Q_RANK, MLA_HEADS * HEAD_PAD).transpose(0, 2, 1)

    wukv = w_ukv.reshape(depth, MLA_KV_RANK, MLA_HEADS, MLA_NOPE_DIM + MLA_V_DIM)
    wuk = jnp.pad(wukv[..., :MLA_NOPE_DIM], ((0, 0), (0, 0), (0, 0), (0, HEAD_PAD - MLA_NOPE_DIM)))
    wuk = wuk.reshape(depth, MLA_KV_RANK, MLA_HEADS * HEAD_PAD)
    place = jnp.zeros((LANES, MLA_HEADS, HEAD_PAD), F32)
    j = jnp.arange(MLA_ROPE_DIM)
    place = place.at[j, :, MLA_NOPE_DIM + j].set(1.0).reshape(LANES, MLA_HEADS * HEAD_PAD)
    wkk = jnp.concatenate([wuk, jnp.broadcast_to(place, (depth,) + place.shape)], axis=1)
    wuv_t = wukv[..., MLA_NOPE_DIM:].reshape(depth, MLA_KV_RANK, MLA_WIDTH).transpose(0, 2, 1)

    return {
        "g_mix": g_mix.reshape(depth, 1, d),
        "wqv_t": jnp.concatenate([cols["q_na"], cols["v_na"]], axis=-1).transpose(0, 2, 1).astype(BF16),
        "wk": cols["k_na"].astype(BF16),
        "wc": wc.astype(BF16),
        "wg": cols["gate"].astype(BF16),
        "b_gate": b_gate.reshape(depth, 1, 2 * d),
        "g_q": g_q.reshape(depth, 1, MLA_Q_RANK),
        "g_kv": g_kv.reshape(depth, 1, MLA_KV_RANK),
        "wuq_t": wuq_t.astype(BF16),
        "wkk": wkk.astype(BF16),
        "wuv_t": wuv_t.astype(BF16),
        "w_br_na": w_br_na.astype(BF16),
        "w_br_mla": w_br_mla.astype(BF16),
        "w_out": w_out.astype(BF16),
        "g_mlp": g_mlp.reshape(depth, 1, d),
        "w1": w_ff1.astype(BF16),
        "w2": w_ff2.astype(BF16),
    }


def _rope_tables(positions):
    half = MLA_ROPE_DIM // 2
    inv_freq = 1.0 / (ROPE_THETA ** (jnp.arange(0, MLA_ROPE_DIM, 2, dtype=F32) / MLA_ROPE_DIM))
    ang = positions.astype(F32)[..., None] * inv_freq
    cos, sin = jnp.cos(ang), jnp.sin(ang)
    pad = jnp.zeros(ang.shape[:-1] + (LANES - 2 * half,), F32)
    cos2 = jnp.concatenate([cos, cos, pad], axis=-1)
    sin2 = jnp.concatenate([-sin, sin, pad], axis=-1)
    return cos.transpose(0, 2, 1), sin.transpose(0, 2, 1), cos2, sin2


def kernel(x, c, positions, w_ada, b_ada, g_mix, w_in, b_gate, rpb, g_q, w_uq, g_kv, w_ukv, w_br_na, w_br_mla,
           w_out, g_mlp, w_ff1, w_ff2, g_final):
    batch, seq, d = x.shape
    depth = w_in.shape[0]
    grid_rows = seq // GRID_W
    assert seq % TOKEN_TILE == 0 and grid_rows % NA_Q_ROWS == 0 and grid_rows >= NA_KEY_ROWS + NA_Q_ROWS

    p = _prepare_params(g_mix, w_in, b_gate, g_q, w_uq, g_kv, w_ukv, w_br_na, w_br_mla, w_out, g_mlp,
                        w_ff1, w_ff2)
    rope = _rope_tables(positions)
    bias = _na_bias_tables(rpb, grid_rows)
    mod = _ada_mod(c, w_ada, b_ada).reshape(depth, batch, 6, d)
    g_fin = g_final.reshape(1, d)

    for layer in range(depth):
        q_na, k_na, v_na, q_m, k_m, v_m, gates = _projections(x, mod[layer], layer, p, rope)
        o_na = _na_attention(q_na, k_na, v_na, bias, layer)
        o_mla = _mla_attention(q_m, k_m, v_m)
        x = _post(x, o_na, o_mla, gates, mod[layer], layer, p, g_fin, layer == depth - 1)
    return x
```

```python
import functools
import math

import numpy as np
import jax
import jax.numpy as jnp
from jax import lax
from jax.experimental import pallas as pl
from jax.experimental.pallas import tpu as pltpu

F32 = jnp.float32
BF16 = jnp.bfloat16

GRID_W = 64
NA_HEADS = 8
NA_HEAD_DIM = 64
NA_WIN_ROWS = 8
NA_WIN_COLS = 16
MLA_HEADS = 8
MLA_Q_RANK = 256
MLA_KV_RANK = 128
MLA_NOPE_DIM = 64
MLA_ROPE_DIM = 32
MLA_V_DIM = 64
ROPE_THETA = 10000.0
NORM_EPS = 1e-6
NA_WIDTH = NA_HEADS * NA_HEAD_DIM
MLA_WIDTH = MLA_HEADS * MLA_V_DIM

LANES = 128
HEAD_PAD = 128
BF16_SUBLANES = 16
V_ROWS = NA_HEAD_DIM + BF16_SUBLANES
LOG2E = math.log2(math.e)
VMEM_LIMIT_BYTES = 56 * 1024 * 1024

TOKEN_TILE = 512
POST_TILE = 256
MLA_Q_TILE = 256
NA_Q_ROWS = 4
NA_KEY_ROWS = 12
NA_Q_TILE = NA_Q_ROWS * GRID_W
NA_KEY_TILE = NA_KEY_ROWS * GRID_W
MASK_VALUE = -1e30


def _const_spec(shape, index_map):
    return pl.BlockSpec(shape, index_map, pipeline_mode=pl.Buffered(1))


def _rms(x, gain):
    return x * lax.rsqrt(jnp.mean(x * x, axis=-1, keepdims=True) + NORM_EPS) * gain


def _nt_dot(a, b):
    return lax.dot_general(a, b, (((1,), (1,)), ((), ())), preferred_element_type=F32)


def _ada_kernel(c_ref, w_ref, b_ref, o_ref):
    c = c_ref[...]
    c_act = (c / (1.0 + jnp.exp(-c))).astype(BF16)
    o_ref[0] = jnp.dot(c_act, w_ref[0].astype(BF16), preferred_element_type=F32) + b_ref[0]


def _ada_mod(c, w_ada, b_ada):
    depth, d, d6 = w_ada.shape
    batch = c.shape[0]
    nchunk = d6 // d
    return pl.pallas_call(
        _ada_kernel,
        grid=(depth, nchunk),
        in_specs=[
            pl.BlockSpec((batch, d), lambda l, j: (0, 0)),
            pl.BlockSpec((1, d, d), lambda l, j: (l, 0, j)),
            pl.BlockSpec((1, 1, d), lambda l, j: (l, 0, j)),
        ],
        out_specs=pl.BlockSpec((1, batch, d), lambda l, j: (l, 0, j)),
        out_shape=jax.ShapeDtypeStruct((depth, batch, d6), F32),
        name="ada_mod",
    )(c, w_ada, b_ada.reshape(depth, 1, d6))


def _proj_kernel(x_ref, mod_ref, gmix_ref, wqv_ref, wk_ref, wc_ref, wg_ref, bg_ref, gq_ref, gkv_ref,
                 wuq_ref, wkk_ref, wuv_ref, cost_ref, sint_ref, cos2_ref, sin2_ref,
                 qna_ref, kna_ref, vna_ref, qm_ref, km_ref, vm_ref, gates_ref):
    tm = x_ref.shape[1]
    x = x_ref[0]
    mod = mod_ref[0]
    shift, scale = mod[0:1], mod[1:2]
    h = (_rms(x, gmix_ref[...]) * (1.0 + scale) + shift).astype(BF16)

    qv_t = _nt_dot(wqv_ref[...], h)
    q_t = (qv_t[:NA_WIDTH] * (NA_HEAD_DIM ** -0.5 * LOG2E)).astype(BF16)
    zeros = jnp.zeros((NA_HEAD_DIM, tm), BF16)
    ones = jnp.ones((V_ROWS - NA_HEAD_DIM, NA_Q_TILE), BF16)
    v_t = qv_t[NA_WIDTH:].astype(BF16)
    for hd in range(NA_HEADS):
        q_h = q_t[hd * NA_HEAD_DIM:(hd + 1) * NA_HEAD_DIM]
        lo, hi = (q_h, zeros) if hd % 2 == 0 else (zeros, q_h)
        qna_ref[0, hd, :NA_HEAD_DIM, :] = lo
        qna_ref[0, hd, NA_HEAD_DIM:, :] = hi
        for ch in range(tm // NA_Q_TILE):
            vna_ref[0, ch, hd * V_ROWS:hd * V_ROWS + NA_HEAD_DIM, :] = (
                v_t[hd * NA_HEAD_DIM:(hd + 1) * NA_HEAD_DIM, ch * NA_Q_TILE:(ch + 1) * NA_Q_TILE])
            vna_ref[0, ch, hd * V_ROWS + NA_HEAD_DIM:(hd + 1) * V_ROWS, :] = ones
    kna_ref[0] = jnp.dot(h, wk_ref[...], preferred_element_type=F32).astype(BF16)

    logits = jnp.dot(h, wg_ref[...], preferred_element_type=F32) + bg_ref[...]
    gates_ref[0] = (1.0 / (1.0 + jnp.exp(-logits))).astype(BF16)

    lat = jnp.dot(h, wc_ref[...], preferred_element_type=F32)
    c_q = lat[:, :MLA_Q_RANK]
    c_kv = lat[:, MLA_Q_RANK:MLA_Q_RANK + MLA_KV_RANK]
    kr = lat[:, MLA_Q_RANK + MLA_KV_RANK:MLA_Q_RANK + MLA_KV_RANK + LANES]
    kr_swapped = lat[:, MLA_Q_RANK + MLA_KV_RANK + LANES:]
    cq_n = _rms(c_q, gq_ref[...]).astype(BF16)
    ckv_n = _rms(c_kv, gkv_ref[...]).astype(BF16)

    q3 = _nt_dot(wuq_ref[...], cq_n).reshape(MLA_HEADS, HEAD_PAD, tm)
    cos_t, sin_t = cost_ref[0], sint_ref[0]
    half = MLA_ROPE_DIM // 2
    x1 = q3[:, MLA_NOPE_DIM:MLA_NOPE_DIM + half]
    x2 = q3[:, MLA_NOPE_DIM + half:MLA_NOPE_DIM + MLA_ROPE_DIM]
    q3 = jnp.concatenate(
        [q3[:, :MLA_NOPE_DIM], x1 * cos_t - x2 * sin_t, x2 * cos_t + x1 * sin_t,
         q3[:, MLA_NOPE_DIM + MLA_ROPE_DIM:]], axis=1)
    qm_ref[0] = (q3 * ((MLA_NOPE_DIM + MLA_ROPE_DIM) ** -0.5 * LOG2E)).astype(BF16)

    k_rot = (kr * cos2_ref[0] + kr_swapped * sin2_ref[0]).astype(BF16)
    k_all = jnp.dot(jnp.concatenate([ckv_n, k_rot], axis=1), wkk_ref[...],
                    preferred_element_type=F32).astype(BF16)
    for hd in range(MLA_HEADS):
        km_ref[0, hd] = k_all[:, hd * HEAD_PAD:(hd + 1) * HEAD_PAD]
    vm_ref[0, :, 0, :MLA_V_DIM, :] = _nt_dot(wuv_ref[...], ckv_n).astype(BF16).reshape(MLA_HEADS, MLA_V_DIM, tm)
    vm_ref[0, :, 0, MLA_V_DIM:, :] = jnp.ones((MLA_HEADS, V_ROWS - MLA_V_DIM, tm), BF16)


def _projections(x, mod_l, layer, p, rope):
    batch, seq, d = x.shape
    tm = TOKEN_TILE
    nt = seq // tm
    cos_t, sin_t, cos2, sin2 = rope
    lat_cols = p["wc"].shape[-1]
    tok = lambda b, t: (b, t, 0)
    per_batch = lambda b, t: (b, 0, 0)
    w3 = lambda b, t: (layer, 0, 0)
    out_shapes = (
        jax.ShapeDtypeStruct((batch, NA_HEADS, HEAD_PAD, seq), BF16),
        jax.ShapeDtypeStruct((batch, seq, NA_WIDTH), BF16),
        jax.ShapeDtypeStruct((batch, seq // NA_Q_TILE, NA_HEADS * V_ROWS, NA_Q_TILE), BF16),
        jax.ShapeDtypeStruct((batch, MLA_HEADS, HEAD_PAD, seq), BF16),
        jax.ShapeDtypeStruct((batch, MLA_HEADS, seq, HEAD_PAD), BF16),
        jax.ShapeDtypeStruct((batch, MLA_HEADS, nt, V_ROWS, tm), BF16),
        jax.ShapeDtypeStruct((batch, seq, 2 * d), BF16),
    )
    out_specs = (
        pl.BlockSpec((1, NA_HEADS, HEAD_PAD, tm), lambda b, t: (b, 0, 0, t)),
        pl.BlockSpec((1, tm, NA_WIDTH), tok),
        pl.BlockSpec((1, tm // NA_Q_TILE, NA_HEADS * V_ROWS, NA_Q_TILE), lambda b, t: (b, t, 0, 0)),
        pl.BlockSpec((1, MLA_HEADS, HEAD_PAD, tm), lambda b, t: (b, 0, 0, t)),
        pl.BlockSpec((1, MLA_HEADS, tm, HEAD_PAD), lambda b, t: (b, 0, t, 0)),
        pl.BlockSpec((1, MLA_HEADS, 1, V_ROWS, tm), lambda b, t: (b, 0, t, 0, 0)),
        pl.BlockSpec((1, tm, 2 * d), tok),
    )
    in_specs = [
        pl.BlockSpec((1, tm, d), tok),
        pl.BlockSpec((1, 6, d), per_batch),
        _const_spec((None, 1, d), w3),
        _const_spec((None, 2 * NA_WIDTH, d), w3),
        _const_spec((None, d, NA_WIDTH), w3),
        _const_spec((None, d, lat_cols), w3),
        _const_spec((None, d, 2 * d), w3),
        _const_spec((None, 1, 2 * d), w3),
        _const_spec((None, 1, MLA_Q_RANK), w3),
        _const_spec((None, 1, MLA_KV_RANK), w3),
        _const_spec((None, MLA_HEADS * HEAD_PAD, MLA_Q_RANK), w3),
        _const_spec((None, MLA_KV_RANK + LANES, MLA_HEADS * HEAD_PAD), w3),
        _const_spec((None, MLA_WIDTH, MLA_KV_RANK), w3),
        pl.BlockSpec((1, MLA_ROPE_DIM // 2, tm), lambda b, t: (b, 0, t)),
        pl.BlockSpec((1, MLA_ROPE_DIM // 2, tm), lambda b, t: (b, 0, t)),
        pl.BlockSpec((1, tm, LANES), tok),
        pl.BlockSpec((1, tm, LANES), tok),
    ]
    return pl.pallas_call(
        _proj_kernel,
        grid=(batch, nt),
        in_specs=in_specs,
        out_specs=out_specs,
        out_shape=out_shapes,
        compiler_params=pltpu.CompilerParams(
            dimension_semantics=("parallel", "parallel"), vmem_limit_bytes=VMEM_LIMIT_BYTES),
        name="projections",
    )(x, mod_l, p["g_mix"], p["wqv_t"], p["wk"], p["wc"], p["wg"], p["b_gate"], p["g_q"], p["g_kv"],
      p["wuq_t"], p["wkk"], p["wuv_t"], cos_t, sin_t, cos2, sin2)


def _na_kernel(q_ref, k_ref, v_ref, bias_ref, o_ref, ot_ref, *, grid_rows):
    i = pl.program_id(1)
    nblk = pl.num_programs(1)
    key_row0 = jnp.clip(NA_Q_ROWS * i - NA_WIN_ROWS // 2, 0, grid_rows - NA_KEY_ROWS)
    key0 = pl.multiple_of(key_row0 * GRID_W, NA_Q_TILE)
    chunk0 = key_row0 // NA_Q_ROWS
    pattern = jnp.where(i == 0, 0, jnp.where(i == nblk - 1, 2, 1))
    nchunk = NA_KEY_TILE // NA_Q_TILE
    for hd in range(NA_HEADS):
        pair = hd // 2
        k_pair = k_ref[0, pl.ds(key0, NA_KEY_TILE), pair * LANES:(pair + 1) * LANES]
        s = jnp.dot(k_pair, q_ref[0, hd], preferred_element_type=F32)
        s = s + bias_ref[pattern, hd]
        m = jnp.max(s, axis=0, keepdims=True)
        p = jnp.exp2(s - m).astype(BF16)
        o_t = jnp.zeros((V_ROWS, NA_Q_TILE), F32)
        for ch in range(nchunk):
            v_t = v_ref[0, chunk0 + ch, hd * V_ROWS:(hd + 1) * V_ROWS, :]
            o_t = o_t + jnp.dot(v_t, p[ch * NA_Q_TILE:(ch + 1) * NA_Q_TILE],
                                preferred_element_type=F32)
        denom = o_t[NA_HEAD_DIM:NA_HEAD_DIM + 1]
        ot_ref[hd * NA_HEAD_DIM:(hd + 1) * NA_HEAD_DIM, :] = o_t[:NA_HEAD_DIM] * (1.0 / denom)
    o_ref[0] = ot_ref[...].T.astype(BF16)


def _na_attention(q_t, k, v_t, bias, layer):
    batch, seq, _ = k.shape
    grid_rows = seq // GRID_W
    nblk = grid_rows // NA_Q_ROWS
    return pl.pallas_call(
        functools.partial(_na_kernel, grid_rows=grid_rows),
        grid=(batch, nblk),
        in_specs=[
            pl.BlockSpec((1, NA_HEADS, HEAD_PAD, NA_Q_TILE), lambda b, i: (b, 0, 0, i)),
            pl.BlockSpec((1, seq, NA_WIDTH), lambda b, i: (b, 0, 0)),
            pl.BlockSpec((1, seq // NA_Q_TILE, NA_HEADS * V_ROWS, NA_Q_TILE), lambda b, i: (b, 0, 0, 0)),
            _const_spec((None, 3, NA_HEADS, NA_KEY_TILE, NA_Q_TILE), lambda b, i: (layer, 0, 0, 0, 0)),
        ],
        out_specs=pl.BlockSpec((1, NA_Q_TILE, NA_WIDTH), lambda b, i: (b, i, 0)),
        out_shape=jax.ShapeDtypeStruct((batch, seq, NA_WIDTH), BF16),
        scratch_shapes=[pltpu.VMEM((NA_WIDTH, NA_Q_TILE), F32)],
        compiler_params=pltpu.CompilerParams(
            dimension_semantics=("parallel", "arbitrary"), vmem_limit_bytes=VMEM_LIMIT_BYTES),
        name="na_attention",
    )(q_t, k, v_t, bias)


def _na_bias_tables(rpb, grid_rows):
    depth = rpb.shape[0]
    nblk = grid_rows // NA_Q_ROWS
    n_col_off = 2 * NA_WIN_COLS - 1
    kc = np.arange(GRID_W)[:, None]
    c = np.arange(GRID_W)[None, :]
    cs = np.clip(c - NA_WIN_COLS // 2, 0, GRID_W - NA_WIN_COLS)
    col_ok = (kc >= cs) & (kc < cs + NA_WIN_COLS)
    col_off = kc - c + (NA_WIN_COLS - 1)
    select = ((col_off[None] == np.arange(n_col_off)[:, None, None]) & col_ok[None])
    select = select.reshape(n_col_off, GRID_W * GRID_W).astype(np.float32)
    col_tiles = jnp.dot((rpb * LOG2E).reshape(-1, n_col_off), select, precision=lax.Precision.HIGHEST)
    col_tiles = col_tiles.reshape(depth, NA_HEADS, 2 * NA_WIN_ROWS - 1, GRID_W, GRID_W)
    col_tiles = jnp.where(col_ok, col_tiles, MASK_VALUE)
    masked = jnp.full((depth, NA_HEADS, GRID_W, GRID_W), MASK_VALUE, F32)
    tables = []
    for i in (0, 1, nblk - 1):
        key_row0 = min(max(NA_Q_ROWS * i - NA_WIN_ROWS // 2, 0), grid_rows - NA_KEY_ROWS)
        rows_k = []
        for kr_rel in range(NA_KEY_ROWS):
            kr = key_row0 + kr_rel
            tiles = []
            for j in range(NA_Q_ROWS):
                r = NA_Q_ROWS * i + j
                rs = min(max(r - NA_WIN_ROWS // 2, 0), grid_rows - NA_WIN_ROWS)
                inside = rs <= kr < rs + NA_WIN_ROWS
                tiles.append(col_tiles[:, :, kr - r + NA_WIN_ROWS - 1] if inside else masked)
            rows_k.append(jnp.stack(tiles, axis=3))
        b = jnp.stack(rows_k, axis=2)
        tables.append(b.reshape(depth, NA_HEADS, NA_KEY_TILE, NA_Q_TILE))
    return jnp.stack(tables, axis=1)


def _mla_kernel(q_ref, k_ref, v_ref, o_ref, s_ref, ot_ref):
    tq = q_ref.shape[-1]
    nk = v_ref.shape[2]
    tk = v_ref.shape[-1]
    sub = 8

    def scores(hd, buf):
        q_t = q_ref[0, hd]
        m_sub = None
        for j in range(nk):
            s = jnp.dot(k_ref[0, hd, j * tk:(j + 1) * tk, :], q_t, preferred_element_type=F32)
            s_ref[buf, j * tk:(j + 1) * tk, :] = s
            m_j = jnp.max(s.reshape(tk // sub, sub, tq), axis=0)
            m_sub = m_j if m_sub is None else jnp.maximum(m_sub, m_j)
        return jnp.max(m_sub, axis=0, keepdims=True)

    def weighted_values(hd, buf, m):
        acc = jnp.zeros((V_ROWS, tq), F32)
        for j in range(nk):
            p = jnp.exp2(s_ref[buf, j * tk:(j + 1) * tk, :] - m).astype(BF16)
            acc = acc + jnp.dot(v_ref[0, hd, j], p, preferred_element_type=F32)
        ot_ref[hd * MLA_V_DIM:(hd + 1) * MLA_V_DIM, :] = (
            acc[:MLA_V_DIM] * (1.0 / acc[MLA_V_DIM:MLA_V_DIM + 1]))

    m = scores(0, 0)
    for hd in range(MLA_HEADS):
        m_next = scores(hd + 1, (hd + 1) % 2) if hd + 1 < MLA_HEADS else None
        weighted_values(hd, hd % 2, m)
        m = m_next
    o_ref[0] = ot_ref[...].T.astype(BF16)


def _mla_attention(q_t, k, v_t):
    batch, _, seq, _ = k.shape
    nk, tk = v_t.shape[2], v_t.shape[4]
    tq = MLA_Q_TILE
    return pl.pallas_call(
        _mla_kernel,
        grid=(batch, seq // tq),
        in_specs=[
            pl.BlockSpec((1, MLA_HEADS, HEAD_PAD, tq), lambda b, i: (b, 0, 0, i)),
            pl.BlockSpec((1, MLA_HEADS, seq, HEAD_PAD), lambda b, i: (b, 0, 0, 0)),
            pl.BlockSpec((1, MLA_HEADS, nk, V_ROWS, tk), lambda b, i: (b, 0, 0, 0, 0)),
        ],
        out_specs=pl.BlockSpec((1, tq, MLA_WIDTH), lambda b, i: (b, i, 0)),
        out_shape=jax.ShapeDtypeStruct((batch, seq, MLA_WIDTH), BF16),
        scratch_shapes=[pltpu.VMEM((2, seq, tq), F32), pltpu.VMEM((MLA_WIDTH, tq), F32)],
        compiler_params=pltpu.CompilerParams(
            dimension_semantics=("parallel", "arbitrary"), vmem_limit_bytes=VMEM_LIMIT_BYTES),
        name="mla_attention",
    )(q_t, k, v_t)


def _post_kernel(x_ref, ona_ref, omla_ref, gates_ref, mod_ref, wbn_ref, wbm_ref, wout_ref, gmlp_ref,
                 w1_ref, w2_ref, gfin_ref, o_ref, *, final):
    d = x_ref.shape[-1]
    x = x_ref[0]
    mod = mod_ref[0]
    gate_a, shift_m, scale_m, gate_m = mod[2:3], mod[3:4], mod[4:5], mod[5:6]
    gates = gates_ref[0]
    br_na = jnp.dot(ona_ref[0], wbn_ref[...], preferred_element_type=F32)
    br_mla = jnp.dot(omla_ref[0], wbm_ref[...], preferred_element_type=F32)
    merged = gates[:, :d].astype(F32) * br_na + gates[:, d:].astype(F32) * br_mla
    x = x + gate_a * jnp.dot(merged.astype(BF16), wout_ref[...], preferred_element_type=F32)

    h = (_rms(x, gmlp_ref[...]) * (1.0 + scale_m) + shift_m).astype(BF16)
    u = jnp.dot(h, w1_ref[...], preferred_element_type=F32)
    u = jnp.square(jnp.maximum(u, 0.0)).astype(BF16)
    x = x + gate_m * jnp.dot(u, w2_ref[...], preferred_element_type=F32)
    if final:
        x = _rms(x, gfin_ref[...])
    o_ref[0] = x


def _post(x, o_na, o_mla, gates, mod_l, layer, p, g_final, final):
    batch, seq, d = x.shape
    d_ff = p["w1"].shape[-1]
    tm = POST_TILE
    tok = lambda b, t: (b, t, 0)
    w3 = lambda b, t: (layer, 0, 0)
    return pl.pallas_call(
        functools.partial(_post_kernel, final=final),
        grid=(batch, seq // tm),
        in_specs=[
            pl.BlockSpec((1, tm, d), tok),
            pl.BlockSpec((1, tm, NA_WIDTH), tok),
            pl.BlockSpec((1, tm, MLA_WIDTH), tok),
            pl.BlockSpec((1, tm, 2 * d), tok),
            pl.BlockSpec((1, 6, d), lambda b, t: (b, 0, 0)),
            _const_spec((None, NA_WIDTH, d), w3),
            _const_spec((None, MLA_WIDTH, d), w3),
            _const_spec((None, d, d), w3),
            _const_spec((None, 1, d), w3),
            _const_spec((None, d, d_ff), w3),
            _const_spec((None, d_ff, d), w3),
            _const_spec((1, d), lambda b, t: (0, 0)),
        ],
        out_specs=pl.BlockSpec((1, tm, d), tok),
        out_shape=jax.ShapeDtypeStruct((batch, seq, d), F32),
        compiler_params=pltpu.CompilerParams(
            dimension_semantics=("parallel", "parallel"), vmem_limit_bytes=VMEM_LIMIT_BYTES),
        name="post",
    )(x, o_na, o_mla, gates, mod_l, p["w_br_na"], p["w_br_mla"], p["w_out"], p["g_mlp"], p["w1"], p["w2"],
      g_final)


def _prepare_params(g_mix, w_in, b_gate, g_q, w_uq, g_kv, w_ukv, w_br_na, w_br_mla, w_out, g_mlp, w_ff1, w_ff2):
    depth, d, _ = w_in.shape
    o = 0
    cols = {}
    for name, size in (("q_na", NA_WIDTH), ("k_na", NA_WIDTH), ("v_na", NA_WIDTH), ("c_q", MLA_Q_RANK),
                       ("c_kv", MLA_KV_RANK), ("k_rope", MLA_ROPE_DIM), ("gate", 2 * d)):
        cols[name] = w_in[:, :, o:o + size]
        o += size
    half = MLA_ROPE_DIM // 2
    rope_pad = jnp.zeros((depth, d, LANES - MLA_ROPE_DIM), F32)
    kr = cols["k_rope"]
    kr_swapped = jnp.concatenate([kr[:, :, half:], kr[:, :, :half]], axis=-1)
    wc = jnp.concatenate([cols["c_q"], cols["c_kv"], kr, rope_pad, kr_swapped, rope_pad], axis=-1)

    qk_dim = MLA_NOPE_DIM + MLA_ROPE_DIM
    wuq = w_uq.reshape(depth, MLA_Q_RANK, MLA_HEADS, qk_dim)
    wuq = jnp.pad(wuq, ((0, 0), (0, 0), (0, 0), (0, HEAD_PAD - qk_dim)))
    wuq_t = wuq.reshape(depth, MLA_Q_RANK, MLA_HEADS * HEAD_PAD).transpose(0, 2, 1)

    wukv = w_ukv.reshape(depth, MLA_KV_RANK, MLA_HEADS, MLA_NOPE_DIM + MLA_V_DIM)
    wuk = jnp.pad(wukv[..., :MLA_NOPE_DIM], ((0, 0), (0, 0), (0, 0), (0, HEAD_PAD - MLA_NOPE_DIM)))
    wuk = wuk.reshape(depth, MLA_KV_RANK, MLA_HEADS * HEAD_PAD)
    place = jnp.zeros((LANES, MLA_HEADS, HEAD_PAD), F32)
    j = jnp.arange(MLA_ROPE_DIM)
    place = place.at[j, :, MLA_NOPE_DIM + j].set(1.0).reshape(LANES, MLA_HEADS * HEAD_PAD)
    wkk = jnp.concatenate([wuk, jnp.broadcast_to(place, (depth,) + place.shape)], axis=1)
    wuv_t = wukv[..., MLA_NOPE_DIM:].reshape(depth, MLA_KV_RANK, MLA_WIDTH).transpose(0, 2, 1)

    return {
        "g_mix": g_mix.reshape(depth, 1, d),
        "wqv_t": jnp.concatenate([cols["q_na"], cols["v_na"]], axis=-1).transpose(0, 2, 1).astype(BF16),
        "wk": cols["k_na"].astype(BF16),
        "wc": wc.astype(BF16),
        "wg": cols["gate"].astype(BF16),
        "b_gate": b_gate.reshape(depth, 1, 2 * d),
        "g_q": g_q.reshape(depth, 1, MLA_Q_RANK),
        "g_kv": g_kv.reshape(depth, 1, MLA_KV_RANK),
        "wuq_t": wuq_t.astype(BF16),
        "wkk": wkk.astype(BF16),
        "wuv_t": wuv_t.astype(BF16),
        "w_br_na": w_br_na.astype(BF16),
        "w_br_mla": w_br_mla.astype(BF16),
        "w_out": w_out.astype(BF16),
        "g_mlp": g_mlp.reshape(depth, 1, d),
        "w1": w_ff1.astype(BF16),
        "w2": w_ff2.astype(BF16),
    }


def _rope_tables(positions):
    half = MLA_ROPE_DIM // 2
    inv_freq = 1.0 / (ROPE_THETA ** (jnp.arange(0, MLA_ROPE_DIM, 2, dtype=F32) / MLA_ROPE_DIM))
    ang = positions.astype(F32)[..., None] * inv_freq
    cos, sin = jnp.cos(ang), jnp.sin(ang)
    pad = jnp.zeros(ang.shape[:-1] + (LANES - 2 * half,), F32)
    cos2 = jnp.concatenate([cos, cos, pad], axis=-1)
    sin2 = jnp.concatenate([-sin, sin, pad], axis=-1)
    return cos.transpose(0, 2, 1), sin.transpose(0, 2, 1), cos2, sin2


def kernel(x, c, positions, w_ada, b_ada, g_mix, w_in, b_gate, rpb, g_q, w_uq, g_kv, w_ukv, w_br_na, w_br_mla,
           w_out, g_mlp, w_ff1, w_ff2, g_final):
    batch, seq, d = x.shape
    depth = w_in.shape[0]
    grid_rows = seq // GRID_W
    assert seq % TOKEN_TILE == 0 and grid_rows % NA_Q_ROWS == 0 and grid_rows >= NA_KEY_ROWS + NA_Q_ROWS

    p = _prepare_params(g_mix, w_in, b_gate, g_q, w_uq, g_kv, w_ukv, w_br_na, w_br_mla, w_out, g_mlp,
                        w_ff1, w_ff2)
    rope = _rope_tables(positions)
    bias = _na_bias_tables(rpb, grid_rows)
    mod = _ada_mod(c, w_ada, b_ada).reshape(depth, batch, 6, d)
    g_fin = g_final.reshape(1, d)

    for layer in range(depth):
        q_na, k_na, v_na, q_m, k_m, v_m, gates = _projections(x, mod[layer], layer, p, rope)
        o_na = _na_attention(q_na, k_na, v_na, bias, layer)
        o_mla = _mla_attention(q_m, k_m, v_m)
        x = _post(x, o_na, o_mla, gates, mod[layer], layer, p, g_fin, layer == depth - 1)
    return x
```

```python
import functools
import math

import numpy as np
import jax
import jax.numpy as jnp
from jax import lax
from jax.experimental import pallas as pl
from jax.experimental.pallas import tpu as pltpu

F32 = jnp.float32
BF16 = jnp.bfloat16

GRID_W = 64
NA_HEADS = 8
NA_HEAD_DIM = 64
NA_WIN_ROWS = 8
NA_WIN_COLS = 16
MLA_HEADS = 8
MLA_Q_RANK = 256
MLA_KV_RANK = 128
MLA_NOPE_DIM = 64
MLA_ROPE_DIM = 32
MLA_V_DIM = 64
ROPE_THETA = 10000.0
NORM_EPS = 1e-6
NA_WIDTH = NA_HEADS * NA_HEAD_DIM
MLA_WIDTH = MLA_HEADS * MLA_V_DIM

LANES = 128
HEAD_PAD = 128
BF16_SUBLANES = 16
V_ROWS = NA_HEAD_DIM + BF16_SUBLANES
LOG2E = math.log2(math.e)
VMEM_LIMIT_BYTES = 56 * 1024 * 1024

TOKEN_TILE = 512
POST_TILE = 256
MLA_Q_TILE = 256
MLA_LOOKAHEAD = 2
NA_Q_ROWS = 4
NA_KEY_ROWS = 12
NA_Q_TILE = NA_Q_ROWS * GRID_W
NA_KEY_TILE = NA_KEY_ROWS * GRID_W
NA_LOOKAHEAD = 3
MASK_VALUE = -1e30


def _const_spec(shape, index_map):
    return pl.BlockSpec(shape, index_map, pipeline_mode=pl.Buffered(1))


def _rms(x, gain):
    return x * lax.rsqrt(jnp.mean(x * x, axis=-1, keepdims=True) + NORM_EPS) * gain


def _nt_dot(a, b):
    return lax.dot_general(a, b, (((1,), (1,)), ((), ())), preferred_element_type=F32)


def _ada_kernel(c_ref, w_ref, b_ref, o_ref):
    c = c_ref[...]
    c_act = (c / (1.0 + jnp.exp(-c))).astype(BF16)
    o_ref[0] = jnp.dot(c_act, w_ref[0].astype(BF16), preferred_element_type=F32) + b_ref[0]


def _ada_mod(c, w_ada, b_ada):
    depth, d, d6 = w_ada.shape
    batch = c.shape[0]
    nchunk = d6 // d
    return pl.pallas_call(
        _ada_kernel,
        grid=(depth, nchunk),
        in_specs=[
            pl.BlockSpec((batch, d), lambda l, j: (0, 0)),
            pl.BlockSpec((1, d, d), lambda l, j: (l, 0, j)),
            pl.BlockSpec((1, 1, d), lambda l, j: (l, 0, j)),
        ],
        out_specs=pl.BlockSpec((1, batch, d), lambda l, j: (l, 0, j)),
        out_shape=jax.ShapeDtypeStruct((depth, batch, d6), F32),
        name="ada_mod",
    )(c, w_ada, b_ada.reshape(depth, 1, d6))


def _proj_kernel(x_ref, mod_ref, gmix_ref, wqv_ref, wk_ref, wc_ref, wg_ref, bg_ref, gq_ref, gkv_ref,
                 wuq_ref, wkk_ref, wuv_ref, cost_ref, sint_ref, cos2_ref, sin2_ref,
                 qna_ref, kna_ref, vna_ref, qm_ref, km_ref, vm_ref, gates_ref):
    tm = x_ref.shape[1]
    x = x_ref[0]
    mod = mod_ref[0]
    shift, scale = mod[0:1], mod[1:2]
    h = (_rms(x, gmix_ref[...]) * (1.0 + scale) + shift).astype(BF16)

    qv_t = _nt_dot(wqv_ref[...], h)
    q_t = (qv_t[:NA_WIDTH] * (NA_HEAD_DIM ** -0.5 * LOG2E)).astype(BF16)
    zeros = jnp.zeros((NA_HEAD_DIM, tm), BF16)
    ones = jnp.ones((V_ROWS - NA_HEAD_DIM, NA_Q_TILE), BF16)
    v_t = qv_t[NA_WIDTH:].astype(BF16)
    for hd in range(NA_HEADS):
        q_h = q_t[hd * NA_HEAD_DIM:(hd + 1) * NA_HEAD_DIM]
        lo, hi = (q_h, zeros) if hd % 2 == 0 else (zeros, q_h)
        qna_ref[0, hd, :NA_HEAD_DIM, :] = lo
        qna_ref[0, hd, NA_HEAD_DIM:, :] = hi
        for ch in range(tm // NA_Q_TILE):
            vna_ref[0, ch, hd * V_ROWS:hd * V_ROWS + NA_HEAD_DIM, :] = (
                v_t[hd * NA_HEAD_DIM:(hd + 1) * NA_HEAD_DIM, ch * NA_Q_TILE:(ch + 1) * NA_Q_TILE])
            vna_ref[0, ch, hd * V_ROWS + NA_HEAD_DIM:(hd + 1) * V_ROWS, :] = ones
    kna_ref[0] = jnp.dot(h, wk_ref[...], preferred_element_type=F32).astype(BF16)

    logits = jnp.dot(h, wg_ref[...], preferred_element_type=F32) + bg_ref[...]
    gates_ref[0] = (1.0 / (1.0 + jnp.exp(-logits))).astype(BF16)

    lat = jnp.dot(h, wc_ref[...], preferred_element_type=F32)
    c_q = lat[:, :MLA_Q_RANK]
    c_kv = lat[:, MLA_Q_RANK:MLA_Q_RANK + MLA_KV_RANK]
    kr = lat[:, MLA_Q_RANK + MLA_KV_RANK:MLA_Q_RANK + MLA_KV_RANK + LANES]
    kr_swapped = lat[:, MLA_Q_RANK + MLA_KV_RANK + LANES:]
    cq_n = _rms(c_q, gq_ref[...]).astype(BF16)
    ckv_n = _rms(c_kv, gkv_ref[...]).astype(BF16)

    q3 = _nt_dot(wuq_ref[...], cq_n).reshape(MLA_HEADS, HEAD_PAD, tm)
    cos_t, sin_t = cost_ref[0], sint_ref[0]
    half = MLA_ROPE_DIM // 2
    x1 = q3[:, MLA_NOPE_DIM:MLA_NOPE_DIM + half]
    x2 = q3[:, MLA_NOPE_DIM + half:MLA_NOPE_DIM + MLA_ROPE_DIM]
    q3 = jnp.concatenate(
        [q3[:, :MLA_NOPE_DIM], x1 * cos_t - x2 * sin_t, x2 * cos_t + x1 * sin_t,
         q3[:, MLA_NOPE_DIM + MLA_ROPE_DIM:]], axis=1)
    qm_ref[0] = (q3 * ((MLA_NOPE_DIM + MLA_ROPE_DIM) ** -0.5 * LOG2E)).astype(BF16)

    k_rot = (kr * cos2_ref[0] + kr_swapped * sin2_ref[0]).astype(BF16)
    k_all = jnp.dot(jnp.concatenate([ckv_n, k_rot], axis=1), wkk_ref[...],
                    preferred_element_type=F32).astype(BF16)
    for hd in range(MLA_HEADS):
        km_ref[0, hd] = k_all[:, hd * HEAD_PAD:(hd + 1) * HEAD_PAD]
    vm_ref[0, :, 0, :MLA_V_DIM, :] = _nt_dot(wuv_ref[...], ckv_n).astype(BF16).reshape(MLA_HEADS, MLA_V_DIM, tm)
    vm_ref[0, :, 0, MLA_V_DIM:, :] = jnp.ones((MLA_HEADS, V_ROWS - MLA_V_DIM, tm), BF16)


def _projections(x, mod_l, layer, p, rope):
    batch, seq, d = x.shape
    tm = TOKEN_TILE
    nt = seq // tm
    cos_t, sin_t, cos2, sin2 = rope
    lat_cols = p["wc"].shape[-1]
    tok = lambda b, t: (b, t, 0)
    per_batch = lambda b, t: (b, 0, 0)
    w3 = lambda b, t: (layer, 0, 0)
    out_shapes = (
        jax.ShapeDtypeStruct((batch, NA_HEADS, HEAD_PAD, seq), BF16),
        jax.ShapeDtypeStruct((batch, seq, NA_WIDTH), BF16),
        jax.ShapeDtypeStruct((batch, seq // NA_Q_TILE, NA_HEADS * V_ROWS, NA_Q_TILE), BF16),
        jax.ShapeDtypeStruct((batch, MLA_HEADS, HEAD_PAD, seq), BF16),
        jax.ShapeDtypeStruct((batch, MLA_HEADS, seq, HEAD_PAD), BF16),
        jax.ShapeDtypeStruct((batch, MLA_HEADS, nt, V_ROWS, tm), BF16),
        jax.ShapeDtypeStruct((batch, seq, 2 * d), BF16),
    )
    out_specs = (
        pl.BlockSpec((1, NA_HEADS, HEAD_PAD, tm), lambda b, t: (b, 0, 0, t)),
        pl.BlockSpec((1, tm, NA_WIDTH), tok),
        pl.BlockSpec((1, tm // NA_Q_TILE, NA_HEADS * V_ROWS, NA_Q_TILE), lambda b, t: (b, t, 0, 0)),
        pl.BlockSpec((1, MLA_HEADS, HEAD_PAD, tm), lambda b, t: (b, 0, 0, t)),
        pl.BlockSpec((1, MLA_HEADS, tm, HEAD_PAD), lambda b, t: (b, 0, t, 0)),
        pl.BlockSpec((1, MLA_HEADS, 1, V_ROWS, tm), lambda b, t: (b, 0, t, 0, 0)),
        pl.BlockSpec((1, tm, 2 * d), tok),
    )
    in_specs = [
        pl.BlockSpec((1, tm, d), tok),
        pl.BlockSpec((1, 6, d), per_batch),
        _const_spec((None, 1, d), w3),
        _const_spec((None, 2 * NA_WIDTH, d), w3),
        _const_spec((None, d, NA_WIDTH), w3),
        _const_spec((None, d, lat_cols), w3),
        _const_spec((None, d, 2 * d), w3),
        _const_spec((None, 1, 2 * d), w3),
        _const_spec((None, 1, MLA_Q_RANK), w3),
        _const_spec((None, 1, MLA_KV_RANK), w3),
        _const_spec((None, MLA_HEADS * HEAD_PAD, MLA_Q_RANK), w3),
        _const_spec((None, MLA_KV_RANK + LANES, MLA_HEADS * HEAD_PAD), w3),
        _const_spec((None, MLA_WIDTH, MLA_KV_RANK), w3),
        pl.BlockSpec((1, MLA_ROPE_DIM // 2, tm), lambda b, t: (b, 0, t)),
        pl.BlockSpec((1, MLA_ROPE_DIM // 2, tm), lambda b, t: (b, 0, t)),
        pl.BlockSpec((1, tm, LANES), tok),
        pl.BlockSpec((1, tm, LANES), tok),
    ]
    return pl.pallas_call(
        _proj_kernel,
        grid=(batch, nt),
        in_specs=in_specs,
        out_specs=out_specs,
        out_shape=out_shapes,
        compiler_params=pltpu.CompilerParams(
            dimension_semantics=("parallel", "parallel"), vmem_limit_bytes=VMEM_LIMIT_BYTES),
        name="projections",
    )(x, mod_l, p["g_mix"], p["wqv_t"], p["wk"], p["wc"], p["wg"], p["b_gate"], p["g_q"], p["g_kv"],
      p["wuq_t"], p["wkk"], p["wuv_t"], cos_t, sin_t, cos2, sin2)


def _na_kernel(q_ref, k_ref, v_ref, bias_ref, o_ref, s_ref, ot_ref, *, grid_rows):
    i = pl.program_id(1)
    nblk = pl.num_programs(1)
    key_row0 = jnp.clip(NA_Q_ROWS * i - NA_WIN_ROWS // 2, 0, grid_rows - NA_KEY_ROWS)
    key0 = pl.multiple_of(key_row0 * GRID_W, NA_Q_TILE)
    chunk0 = key_row0 // NA_Q_ROWS
    pattern = jnp.where(i == 0, 0, jnp.where(i == nblk - 1, 2, 1))
    nchunk = NA_KEY_TILE // NA_Q_TILE
    sub = 8

    def scores(hd, buf):
        pair = hd // 2
        k_pair = k_ref[0, pl.ds(key0, NA_KEY_TILE), pair * LANES:(pair + 1) * LANES]
        s = jnp.dot(k_pair, q_ref[0, hd], preferred_element_type=F32)
        s = s + bias_ref[pattern, hd]
        s_ref[buf] = s
        m_sub = jnp.max(s.reshape(NA_KEY_TILE // sub, sub, NA_Q_TILE), axis=0)
        return jnp.max(m_sub, axis=0, keepdims=True)

    def weighted_values(hd, buf, m):
        o_t = jnp.zeros((V_ROWS, NA_Q_TILE), F32)
        for ch in range(nchunk):
            p = jnp.exp2(s_ref[buf, ch * NA_Q_TILE:(ch + 1) * NA_Q_TILE, :] - m).astype(BF16)
            v_t = v_ref[0, chunk0 + ch, hd * V_ROWS:(hd + 1) * V_ROWS, :]
            o_t = o_t + jnp.dot(v_t, p, preferred_element_type=F32)
        denom = o_t[NA_HEAD_DIM:NA_HEAD_DIM + 1]
        ot_ref[hd * NA_HEAD_DIM:(hd + 1) * NA_HEAD_DIM, :] = o_t[:NA_HEAD_DIM] * (1.0 / denom)

    nbuf = s_ref.shape[0]
    maxima = [scores(hd, hd % nbuf) for hd in range(NA_LOOKAHEAD)]
    for hd in range(NA_HEADS):
        ahead = hd + NA_LOOKAHEAD
        if ahead < NA_HEADS:
            maxima.append(scores(ahead, ahead % nbuf))
        weighted_values(hd, hd % nbuf, maxima[hd])
    o_ref[0] = ot_ref[...].T.astype(BF16)


def _na_attention(q_t, k, v_t, bias, layer):
    batch, seq, _ = k.shape
    grid_rows = seq // GRID_W
    nblk = grid_rows // NA_Q_ROWS
    return pl.pallas_call(
        functools.partial(_na_kernel, grid_rows=grid_rows),
        grid=(batch, nblk),
        in_specs=[
            pl.BlockSpec((1, NA_HEADS, HEAD_PAD, NA_Q_TILE), lambda b, i: (b, 0, 0, i)),
            pl.BlockSpec((1, seq, NA_WIDTH), lambda b, i: (b, 0, 0)),
            pl.BlockSpec((1, seq // NA_Q_TILE, NA_HEADS * V_ROWS, NA_Q_TILE), lambda b, i: (b, 0, 0, 0)),
            _const_spec((None, 3, NA_HEADS, NA_KEY_TILE, NA_Q_TILE), lambda b, i: (layer, 0, 0, 0, 0)),
        ],
        out_specs=pl.BlockSpec((1, NA_Q_TILE, NA_WIDTH), lambda b, i: (b, i, 0)),
        out_shape=jax.ShapeDtypeStruct((batch, seq, NA_WIDTH), BF16),
        scratch_shapes=[pltpu.VMEM((NA_LOOKAHEAD + 1, NA_KEY_TILE, NA_Q_TILE), F32),
                        pltpu.VMEM((NA_WIDTH, NA_Q_TILE), F32)],
        compiler_params=pltpu.CompilerParams(
            dimension_semantics=("parallel", "arbitrary"), vmem_limit_bytes=VMEM_LIMIT_BYTES),
        name="na_attention",
    )(q_t, k, v_t, bias)


def _na_bias_tables(rpb, grid_rows):
    depth = rpb.shape[0]
    nblk = grid_rows // NA_Q_ROWS
    n_col_off = 2 * NA_WIN_COLS - 1
    kc = np.arange(GRID_W)[:, None]
    c = np.arange(GRID_W)[None, :]
    cs = np.clip(c - NA_WIN_COLS // 2, 0, GRID_W - NA_WIN_COLS)
    col_ok = (kc >= cs) & (kc < cs + NA_WIN_COLS)
    col_off = kc - c + (NA_WIN_COLS - 1)
    select = ((col_off[None] == np.arange(n_col_off)[:, None, None]) & col_ok[None])
    select = select.reshape(n_col_off, GRID_W * GRID_W).astype(np.float32)
    col_tiles = jnp.dot((rpb * LOG2E).reshape(-1, n_col_off), select, precision=lax.Precision.HIGHEST)
    col_tiles = col_tiles.reshape(depth, NA_HEADS, 2 * NA_WIN_ROWS - 1, GRID_W, GRID_W)
    col_tiles = jnp.where(col_ok, col_tiles, MASK_VALUE)
    masked = jnp.full((depth, NA_HEADS, GRID_W, GRID_W), MASK_VALUE, F32)
    tables = []
    for i in (0, 1, nblk - 1):
        key_row0 = min(max(NA_Q_ROWS * i - NA_WIN_ROWS // 2, 0), grid_rows - NA_KEY_ROWS)
        rows_k = []
        for kr_rel in range(NA_KEY_ROWS):
            kr = key_row0 + kr_rel
            tiles = []
            for j in range(NA_Q_ROWS):
                r = NA_Q_ROWS * i + j
                rs = min(max(r - NA_WIN_ROWS // 2, 0), grid_rows - NA_WIN_ROWS)
                inside = rs <= kr < rs + NA_WIN_ROWS
                tiles.append(col_tiles[:, :, kr - r + NA_WIN_ROWS - 1] if inside else masked)
            rows_k.append(jnp.stack(tiles, axis=3))
        b = jnp.stack(rows_k, axis=2)
        tables.append(b.reshape(depth, NA_HEADS, NA_KEY_TILE, NA_Q_TILE))
    return jnp.stack(tables, axis=1)


def _mla_kernel(q_ref, k_ref, v_ref, o_ref, s_ref, ot_ref):
    tq = q_ref.shape[-1]
    nk = v_ref.shape[2]
    tk = v_ref.shape[-1]
    sub = 8

    def scores(hd, buf):
        q_t = q_ref[0, hd]
        m_sub = None
        for j in range(nk):
            s = jnp.dot(k_ref[0, hd, j * tk:(j + 1) * tk, :], q_t, preferred_element_type=F32)
            s_ref[buf, j * tk:(j + 1) * tk, :] = s
            m_j = jnp.max(s.reshape(tk // sub, sub, tq), axis=0)
            m_sub = m_j if m_sub is None else jnp.maximum(m_sub, m_j)
        return jnp.max(m_sub, axis=0, keepdims=True)

    def weighted_values(hd, buf, m):
        acc = jnp.zeros((V_ROWS, tq), F32)
        for j in range(nk):
            p = jnp.exp2(s_ref[buf, j * tk:(j + 1) * tk, :] - m).astype(BF16)
            acc = acc + jnp.dot(v_ref[0, hd, j], p, preferred_element_type=F32)
        ot_ref[hd * MLA_V_DIM:(hd + 1) * MLA_V_DIM, :] = (
            acc[:MLA_V_DIM] * (1.0 / acc[MLA_V_DIM:MLA_V_DIM + 1]))

    nbuf = s_ref.shape[0]
    maxima = [scores(hd, hd % nbuf) for hd in range(MLA_LOOKAHEAD)]
    for hd in range(MLA_HEADS):
        ahead = hd + MLA_LOOKAHEAD
        if ahead < MLA_HEADS:
            maxima.append(scores(ahead, ahead % nbuf))
        weighted_values(hd, hd % nbuf, maxima[hd])
    o_ref[0] = ot_ref[...].T.astype(BF16)


def _mla_attention(q_t, k, v_t):
    batch, _, seq, _ = k.shape
    nk, tk = v_t.shape[2], v_t.shape[4]
    tq = MLA_Q_TILE
    return pl.pallas_call(
        _mla_kernel,
        grid=(batch, seq // tq),
        in_specs=[
            pl.BlockSpec((1, MLA_HEADS, HEAD_PAD, tq), lambda b, i: (b, 0, 0, i)),
            pl.BlockSpec((1, MLA_HEADS, seq, HEAD_PAD), lambda b, i: (b, 0, 0, 0)),
            pl.BlockSpec((1, MLA_HEADS, nk, V_ROWS, tk), lambda b, i: (b, 0, 0, 0, 0)),
        ],
        out_specs=pl.BlockSpec((1, tq, MLA_WIDTH), lambda b, i: (b, i, 0)),
        out_shape=jax.ShapeDtypeStruct((batch, seq, MLA_WIDTH), BF16),
        scratch_shapes=[pltpu.VMEM((MLA_LOOKAHEAD + 1, seq, tq), F32), pltpu.VMEM((MLA_WIDTH, tq), F32)],
        compiler_params=pltpu.CompilerParams(
            dimension_semantics=("parallel", "arbitrary"), vmem_limit_bytes=VMEM_LIMIT_BYTES),
        name="mla_attention",
    )(q_t, k, v_t)


def _post_kernel(x_ref, ona_ref, omla_ref, gates_ref, mod_ref, wbn_ref, wbm_ref, wout_ref, gmlp_ref,
                 w1_ref, w2_ref, gfin_ref, o_ref, *, final):
    d = x_ref.shape[-1]
    x = x_ref[0]
    mod = mod_ref[0]
    gate_a, shift_m, scale_m, gate_m = mod[2:3], mod[3:4], mod[4:5], mod[5:6]
    gates = gates_ref[0]
    br_na = jnp.dot(ona_ref[0], wbn_ref[...], preferred_element_type=F32)
    br_mla = jnp.dot(omla_ref[0], wbm_ref[...], preferred_element_type=F32)
    merged = gates[:, :d].astype(F32) * br_na + gates[:, d:].astype(F32) * br_mla
    x = x + gate_a * jnp.dot(merged.astype(BF16), wout_ref[...], preferred_element_type=F32)

    h = (_rms(x, gmlp_ref[...]) * (1.0 + scale_m) + shift_m).astype(BF16)
    u = jnp.dot(h, w1_ref[...], preferred_element_type=F32)
    u = jnp.square(jnp.maximum(u, 0.0)).astype(BF16)
    x = x + gate_m * jnp.dot(u, w2_ref[...], preferred_element_type=F32)
    if final:
        x = _rms(x, gfin_ref[...])
    o_ref[0] = x


def _post(x, o_na, o_mla, gates, mod_l, layer, p, g_final, final):
    batch, seq, d = x.shape
    d_ff = p["w1"].shape[-1]
    tm = POST_TILE
    tok = lambda b, t: (b, t, 0)
    w3 = lambda b, t: (layer, 0, 0)
    return pl.pallas_call(
        functools.partial(_post_kernel, final=final),
        grid=(batch, seq // tm),
        in_specs=[
            pl.BlockSpec((1, tm, d), tok),
            pl.BlockSpec((1, tm, NA_WIDTH), tok),
            pl.BlockSpec((1, tm, MLA_WIDTH), tok),
            pl.BlockSpec((1, tm, 2 * d), tok),
            pl.BlockSpec((1, 6, d), lambda b, t: (b, 0, 0)),
            _const_spec((None, NA_WIDTH, d), w3),
            _const_spec((None, MLA_WIDTH, d), w3),
            _const_spec((None, d, d), w3),
            _const_spec((None, 1, d), w3),
            _const_spec((None, d, d_ff), w3),
            _const_spec((None, d_ff, d), w3),
            _const_spec((1, d), lambda b, t: (0, 0)),
        ],
        out_specs=pl.BlockSpec((1, tm, d), tok),
        out_shape=jax.ShapeDtypeStruct((batch, seq, d), F32),
        compiler_params=pltpu.CompilerParams(
            dimension_semantics=("parallel", "parallel"), vmem_limit_bytes=VMEM_LIMIT_BYTES),
        name="post",
    )(x, o_na, o_mla, gates, mod_l, p["w_br_na"], p["w_br_mla"], p["w_out"], p["g_mlp"], p["w1"], p["w2"],
      g_final)


def _prepare_params(g_mix, w_in, b_gate, g_q, w_uq, g_kv, w_ukv, w_br_na, w_br_mla, w_out, g_mlp, w_ff1, w_ff2):
    depth, d, _ = w_in.shape
    o = 0
    cols = {}
    for name, size in (("q_na", NA_WIDTH), ("k_na", NA_WIDTH), ("v_na", NA_WIDTH), ("c_q", MLA_Q_RANK),
                       ("c_kv", MLA_KV_RANK), ("k_rope", MLA_ROPE_DIM), ("gate", 2 * d)):
        cols[name] = w_in[:, :, o:o + size]
        o += size
    half = MLA_ROPE_DIM // 2
    rope_pad = jnp.zeros((depth, d, LANES - MLA_ROPE_DIM), F32)
    kr = cols["k_rope"]
    kr_swapped = jnp.concatenate([kr[:, :, half:], kr[:, :, :half]], axis=-1)
    wc = jnp.concatenate([cols["c_q"], cols["c_kv"], kr, rope_pad, kr_swapped, rope_pad], axis=-1)

    qk_dim = MLA_NOPE_DIM + MLA_ROPE_DIM
    wuq = w_uq.reshape(depth, MLA_Q_RANK, MLA_HEADS, qk_dim)
    wuq = jnp.pad(wuq, ((0, 0), (0, 0), (0, 0), (0, HEAD_PAD - qk_dim)))
    wuq_t = wuq.reshape(depth, MLA_Q_RANK, MLA_HEADS * HEAD_PAD).transpose(0, 2, 1)

    wukv = w_ukv.reshape(depth, MLA_KV_RANK, MLA_HEADS, MLA_NOPE_DIM + MLA_V_DIM)
    wuk = jnp.pad(wukv[..., :MLA_NOPE_DIM], ((0, 0), (0, 0), (0, 0), (0, HEAD_PAD - MLA_NOPE_DIM)))
    wuk = wuk.reshape(depth, MLA_KV_RANK, MLA_HEADS * HEAD_PAD)
    place = jnp.zeros((LANES, MLA_HEADS, HEAD_PAD), F32)
    j = jnp.arange(MLA_ROPE_DIM)
    place = place.at[j, :, MLA_NOPE_DIM + j].set(1.0).reshape(LANES, MLA_HEADS * HEAD_PAD)
    wkk = jnp.concatenate([wuk, jnp.broadcast_to(place, (depth,) + place.shape)], axis=1)
    wuv_t = wukv[..., MLA_NOPE_DIM:].reshape(depth, MLA_KV_RANK, MLA_WIDTH).transpose(0, 2, 1)

    return {
        "g_mix": g_mix.reshape(depth, 1, d),
        "wqv_t": jnp.concatenate([cols["q_na"], cols["v_na"]], axis=-1).transpose(0, 2, 1).astype(BF16),
        "wk": cols["k_na"].astype(BF16),
        "wc": wc.astype(BF16),
        "wg": cols["gate"].astype(BF16),
        "b_gate": b_gate.reshape(depth, 1, 2 * d),
        "g_q": g_q.reshape(depth, 1, MLA_Q_RANK),
        "g_kv": g_kv.reshape(depth, 1, MLA_KV_RANK),
        "wuq_t": wuq_t.astype(BF16),
        "wkk": wkk.astype(BF16),
        "wuv_t": wuv_t.astype(BF16),
        "w_br_na": w_br_na.astype(BF16),
        "w_br_mla": w_br_mla.astype(BF16),
        "w_out": w_out.astype(BF16),
        "g_mlp": g_mlp.reshape(depth, 1, d),
        "w1": w_ff1.astype(BF16),
        "w2": w_ff2.astype(BF16),
    }


def _rope_tables(positions):
    half = MLA_ROPE_DIM // 2
    inv_freq = 1.0 / (ROPE_THETA ** (jnp.arange(0, MLA_ROPE_DIM, 2, dtype=F32) / MLA_ROPE_DIM))
    ang = positions.astype(F32)[..., None] * inv_freq
    cos, sin = jnp.cos(ang), jnp.sin(ang)
    pad = jnp.zeros(ang.shape[:-1] + (LANES - 2 * half,), F32)
    cos2 = jnp.concatenate([cos, cos, pad], axis=-1)
    sin2 = jnp.concatenate([-sin, sin, pad], axis=-1)
    return cos.transpose(0, 2, 1), sin.transpose(0, 2, 1), cos2, sin2


def kernel(x, c, positions, w_ada, b_ada, g_mix, w_in, b_gate, rpb, g_q, w_uq, g_kv, w_ukv, w_br_na, w_br_mla,
           w_out, g_mlp, w_ff1, w_ff2, g_final):
    batch, seq, d = x.shape
    depth = w_in.shape[0]
    grid_rows = seq // GRID_W
    assert seq % TOKEN_TILE == 0 and grid_rows % NA_Q_ROWS == 0 and grid_rows >= NA_KEY_ROWS + NA_Q_ROWS

    p = _prepare_params(g_mix, w_in, b_gate, g_q, w_uq, g_kv, w_ukv, w_br_na, w_br_mla, w_out, g_mlp,
                        w_ff1, w_ff2)
    rope = _rope_tables(positions)
    bias = _na_bias_tables(rpb, grid_rows)
    mod = _ada_mod(c, w_ada, b_ada).reshape(depth, batch, 6, d)
    g_fin = g_final.reshape(1, d)

    for layer in range(depth):
        q_na, k_na, v_na, q_m, k_m, v_m, gates = _projections(x, mod[layer], layer, p, rope)
        o_na = _na_attention(q_na, k_na, v_na, bias, layer)
        o_mla = _mla_attention(q_m, k_m, v_m)
        x = _post(x, o_na, o_mla, gates, mod[layer], layer, p, g_fin, layer == depth - 1)
    return x
```

```python
import functools
import math

import numpy as np
import jax
import jax.numpy as jnp
from jax import lax
from jax.experimental import pallas as pl
from jax.experimental.pallas import tpu as pltpu

F32 = jnp.float32
BF16 = jnp.bfloat16

GRID_W = 64
NA_HEADS = 8
NA_HEAD_DIM = 64
NA_WIN_ROWS = 8
NA_WIN_COLS = 16
MLA_HEADS = 8
MLA_Q_RANK = 256
MLA_KV_RANK = 128
MLA_NOPE_DIM = 64
MLA_ROPE_DIM = 32
MLA_V_DIM = 64
ROPE_THETA = 10000.0
NORM_EPS = 1e-6
NA_WIDTH = NA_HEADS * NA_HEAD_DIM
MLA_WIDTH = MLA_HEADS * MLA_V_DIM

LANES = 128
HEAD_PAD = 128
BF16_SUBLANES = 16
V_ROWS = NA_HEAD_DIM + BF16_SUBLANES
LOG2E = math.log2(math.e)
VMEM_LIMIT_BYTES = 56 * 1024 * 1024

TOKEN_TILE = 512
POST_TILE = 256
MLA_Q_TILE = 256
MLA_LOOKAHEAD = 3
NA_Q_ROWS = 4
NA_KEY_ROWS = 12
NA_Q_TILE = NA_Q_ROWS * GRID_W
NA_KEY_TILE = NA_KEY_ROWS * GRID_W
NA_LOOKAHEAD = 3
MASK_VALUE = -1e30


def _const_spec(shape, index_map):
    return pl.BlockSpec(shape, index_map, pipeline_mode=pl.Buffered(1))


def _rms(x, gain):
    return x * lax.rsqrt(jnp.mean(x * x, axis=-1, keepdims=True) + NORM_EPS) * gain


def _nt_dot(a, b):
    return lax.dot_general(a, b, (((1,), (1,)), ((), ())), preferred_element_type=F32)


def _ada_kernel(c_ref, w_ref, b_ref, o_ref):
    c = c_ref[...]
    c_act = (c / (1.0 + jnp.exp(-c))).astype(BF16)
    o_ref[0] = jnp.dot(c_act, w_ref[0].astype(BF16), preferred_element_type=F32) + b_ref[0]


def _ada_mod(c, w_ada, b_ada):
    depth, d, d6 = w_ada.shape
    batch = c.shape[0]
    nchunk = d6 // d
    return pl.pallas_call(
        _ada_kernel,
        grid=(depth, nchunk),
        in_specs=[
            pl.BlockSpec((batch, d), lambda l, j: (0, 0)),
            pl.BlockSpec((1, d, d), lambda l, j: (l, 0, j)),
            pl.BlockSpec((1, 1, d), lambda l, j: (l, 0, j)),
        ],
        out_specs=pl.BlockSpec((1, batch, d), lambda l, j: (l, 0, j)),
        out_shape=jax.ShapeDtypeStruct((depth, batch, d6), F32),
        name="ada_mod",
    )(c, w_ada, b_ada.reshape(depth, 1, d6))


def _proj_kernel(x_ref, mod_ref, gmix_ref, wqv_ref, wk_ref, wc_ref, wg_ref, bg_ref, gq_ref, gkv_ref,
                 wuq_ref, wkk_ref, wuv_ref, cost_ref, sint_ref, cos2_ref, sin2_ref,
                 qna_ref, kna_ref, vna_ref, qm_ref, km_ref, vm_ref, gates_ref):
    tm = x_ref.shape[1]
    x = x_ref[0]
    mod = mod_ref[0]
    shift, scale = mod[0:1], mod[1:2]
    h = (_rms(x, gmix_ref[...]) * (1.0 + scale) + shift).astype(BF16)

    qv_t = _nt_dot(wqv_ref[...], h)
    q_t = (qv_t[:NA_WIDTH] * (NA_HEAD_DIM ** -0.5 * LOG2E)).astype(BF16)
    zeros = jnp.zeros((NA_HEAD_DIM, tm), BF16)
    ones = jnp.ones((V_ROWS - NA_HEAD_DIM, NA_Q_TILE), BF16)
    v_t = qv_t[NA_WIDTH:].astype(BF16)
    for hd in range(NA_HEADS):
        q_h = q_t[hd * NA_HEAD_DIM:(hd + 1) * NA_HEAD_DIM]
        lo, hi = (q_h, zeros) if hd % 2 == 0 else (zeros, q_h)
        qna_ref[0, hd, :NA_HEAD_DIM, :] = lo
        qna_ref[0, hd, NA_HEAD_DIM:, :] = hi
        for ch in range(tm // NA_Q_TILE):
            vna_ref[0, ch, hd * V_ROWS:hd * V_ROWS + NA_HEAD_DIM, :] = (
                v_t[hd * NA_HEAD_DIM:(hd + 1) * NA_HEAD_DIM, ch * NA_Q_TILE:(ch + 1) * NA_Q_TILE])
            vna_ref[0, ch, hd * V_ROWS + NA_HEAD_DIM:(hd + 1) * V_ROWS, :] = ones
    kna_ref[0] = jnp.dot(h, wk_ref[...], preferred_element_type=F32).astype(BF16)

    logits = jnp.dot(h, wg_ref[...], preferred_element_type=F32) + bg_ref[...]
    gates_ref[0] = (1.0 / (1.0 + jnp.exp(-logits))).astype(BF16)

    lat = jnp.dot(h, wc_ref[...], preferred_element_type=F32)
    c_q = lat[:, :MLA_Q_RANK]
    c_kv = lat[:, MLA_Q_RANK:MLA_Q_RANK + MLA_KV_RANK]
    kr = lat[:, MLA_Q_RANK + MLA_KV_RANK:MLA_Q_RANK + MLA_KV_RANK + LANES]
    kr_swapped = lat[:, MLA_Q_RANK + MLA_KV_RANK + LANES:]
    cq_n = _rms(c_q, gq_ref[...]).astype(BF16)
    ckv_n = _rms(c_kv, gkv_ref[...]).astype(BF16)

    q3 = _nt_dot(wuq_ref[...], cq_n).reshape(MLA_HEADS, HEAD_PAD, tm)
    cos_t, sin_t = cost_ref[0], sint_ref[0]
    half = MLA_ROPE_DIM // 2
    x1 = q3[:, MLA_NOPE_DIM:MLA_NOPE_DIM + half]
    x2 = q3[:, MLA_NOPE_DIM + half:MLA_NOPE_DIM + MLA_ROPE_DIM]
    q3 = jnp.concatenate(
        [q3[:, :MLA_NOPE_DIM], x1 * cos_t - x2 * sin_t, x2 * cos_t + x1 * sin_t,
         q3[:, MLA_NOPE_DIM + MLA_ROPE_DIM:]], axis=1)
    qm_ref[0] = (q3 * ((MLA_NOPE_DIM + MLA_ROPE_DIM) ** -0.5 * LOG2E)).astype(BF16)

    k_rot = (kr * cos2_ref[0] + kr_swapped * sin2_ref[0]).astype(BF16)
    k_all = jnp.dot(jnp.concatenate([ckv_n, k_rot], axis=1), wkk_ref[...],
                    preferred_element_type=F32).astype(BF16)
    for hd in range(MLA_HEADS):
        km_ref[0, hd] = k_all[:, hd * HEAD_PAD:(hd + 1) * HEAD_PAD]
    vm_ref[0, :, 0, :MLA_V_DIM, :] = _nt_dot(wuv_ref[...], ckv_n).astype(BF16).reshape(MLA_HEADS, MLA_V_DIM, tm)
    vm_ref[0, :, 0, MLA_V_DIM:, :] = jnp.ones((MLA_HEADS, V_ROWS - MLA_V_DIM, tm), BF16)


def _projections(x, mod_l, layer, p, rope):
    batch, seq, d = x.shape
    tm = TOKEN_TILE
    nt = seq // tm
    cos_t, sin_t, cos2, sin2 = rope
    lat_cols = p["wc"].shape[-1]
    tok = lambda b, t: (b, t, 0)
    per_batch = lambda b, t: (b, 0, 0)
    w3 = lambda b, t: (layer, 0, 0)
    out_shapes = (
        jax.ShapeDtypeStruct((batch, NA_HEADS, HEAD_PAD, seq), BF16),
        jax.ShapeDtypeStruct((batch, seq, NA_WIDTH), BF16),
        jax.ShapeDtypeStruct((batch, seq // NA_Q_TILE, NA_HEADS * V_ROWS, NA_Q_TILE), BF16),
        jax.ShapeDtypeStruct((batch, MLA_HEADS, HEAD_PAD, seq), BF16),
        jax.ShapeDtypeStruct((batch, MLA_HEADS, seq, HEAD_PAD), BF16),
        jax.ShapeDtypeStruct((batch, MLA_HEADS, nt, V_ROWS, tm), BF16),
        jax.ShapeDtypeStruct((batch, seq, 2 * d), BF16),
    )
    out_specs = (
        pl.BlockSpec((1, NA_HEADS, HEAD_PAD, tm), lambda b, t: (b, 0, 0, t)),
        pl.BlockSpec((1, tm, NA_WIDTH), tok),
        pl.BlockSpec((1, tm // NA_Q_TILE, NA_HEADS * V_ROWS, NA_Q_TILE), lambda b, t: (b, t, 0, 0)),
        pl.BlockSpec((1, MLA_HEADS, HEAD_PAD, tm), lambda b, t: (b, 0, 0, t)),
        pl.BlockSpec((1, MLA_HEADS, tm, HEAD_PAD), lambda b, t: (b, 0, t, 0)),
        pl.BlockSpec((1, MLA_HEADS, 1, V_ROWS, tm), lambda b, t: (b, 0, t, 0, 0)),
        pl.BlockSpec((1, tm, 2 * d), tok),
    )
    in_specs = [
        pl.BlockSpec((1, tm, d), tok),
        pl.BlockSpec((1, 6, d), per_batch),
        _const_spec((None, 1, d), w3),
        _const_spec((None, 2 * NA_WIDTH, d), w3),
        _const_spec((None, d, NA_WIDTH), w3),
        _const_spec((None, d, lat_cols), w3),
        _const_spec((None, d, 2 * d), w3),
        _const_spec((None, 1, 2 * d), w3),
        _const_spec((None, 1, MLA_Q_RANK), w3),
        _const_spec((None, 1, MLA_KV_RANK), w3),
        _const_spec((None, MLA_HEADS * HEAD_PAD, MLA_Q_RANK), w3),
        _const_spec((None, MLA_KV_RANK + LANES, MLA_HEADS * HEAD_PAD), w3),
        _const_spec((None, MLA_WIDTH, MLA_KV_RANK), w3),
        pl.BlockSpec((1, MLA_ROPE_DIM // 2, tm), lambda b, t: (b, 0, t)),
        pl.BlockSpec((1, MLA_ROPE_DIM // 2, tm), lambda b, t: (b, 0, t)),
        pl.BlockSpec((1, tm, LANES), tok),
        pl.BlockSpec((1, tm, LANES), tok),
    ]
    return pl.pallas_call(
        _proj_kernel,
        grid=(batch, nt),
        in_specs=in_specs,
        out_specs=out_specs,
        out_shape=out_shapes,
        compiler_params=pltpu.CompilerParams(
            dimension_semantics=("parallel", "parallel"), vmem_limit_bytes=VMEM_LIMIT_BYTES),
        name="projections",
    )(x, mod_l, p["g_mix"], p["wqv_t"], p["wk"], p["wc"], p["wg"], p["b_gate"], p["g_q"], p["g_kv"],
      p["wuq_t"], p["wkk"], p["wuv_t"], cos_t, sin_t, cos2, sin2)


def _na_kernel(q_ref, qnext_ref, k_ref, v_ref, bias_ref, o_ref, s_ref, m_ref, ot_ref, *, grid_rows):
    i = pl.program_id(1)
    nblk = pl.num_programs(1)
    nchunk = NA_KEY_TILE // NA_Q_TILE
    sub = 8
    nbuf = s_ref.shape[0]
    lookahead = nbuf - 1

    def window(step):
        key_row0 = jnp.clip(NA_Q_ROWS * step - NA_WIN_ROWS // 2, 0, grid_rows - NA_KEY_ROWS)
        pattern = jnp.where(step == 0, 0, jnp.where(step == nblk - 1, 2, 1))
        return pl.multiple_of(key_row0 * GRID_W, NA_Q_TILE), key_row0 // NA_Q_ROWS, pattern

    def scores(src_ref, win, hd):
        key0, _, pattern = win
        pair = hd // 2
        k_pair = k_ref[0, pl.ds(key0, NA_KEY_TILE), pair * LANES:(pair + 1) * LANES]
        s = jnp.dot(k_pair, src_ref[0, hd], preferred_element_type=F32)
        s = s + bias_ref[pattern, hd]
        s_ref[hd % nbuf] = s
        m_sub = jnp.max(s.reshape(NA_KEY_TILE // sub, sub, NA_Q_TILE), axis=0)
        return jnp.max(m_sub, axis=0, keepdims=True)

    def weighted_values(win, hd, m):
        _, chunk0, _ = win
        o_t = jnp.zeros((V_ROWS, NA_Q_TILE), F32)
        for ch in range(nchunk):
            p = jnp.exp2(s_ref[hd % nbuf, ch * NA_Q_TILE:(ch + 1) * NA_Q_TILE, :] - m).astype(BF16)
            v_t = v_ref[0, chunk0 + ch, hd * V_ROWS:(hd + 1) * V_ROWS, :]
            o_t = o_t + jnp.dot(v_t, p, preferred_element_type=F32)
        denom = o_t[NA_HEAD_DIM:NA_HEAD_DIM + 1]
        ot_ref[hd * NA_HEAD_DIM:(hd + 1) * NA_HEAD_DIM, :] = o_t[:NA_HEAD_DIM] * (1.0 / denom)

    cur = window(i)
    nxt = window(jnp.minimum(i + 1, nblk - 1))

    @pl.when(i == 0)
    def _first_block_of_batch():
        for hd in range(lookahead):
            m_ref[hd:hd + 1, :] = scores(q_ref, cur, hd)

    maxima = {hd: m_ref[hd:hd + 1, :] for hd in range(lookahead)}
    for hd in range(NA_HEADS):
        ahead = hd + lookahead
        if ahead < NA_HEADS:
            maxima[ahead] = scores(q_ref, cur, ahead)
        else:
            m_ref[ahead - NA_HEADS:ahead - NA_HEADS + 1, :] = scores(qnext_ref, nxt, ahead - NA_HEADS)
        weighted_values(cur, hd, maxima[hd])
    o_ref[0] = ot_ref[...].T.astype(BF16)


def _na_attention(q_t, k, v_t, bias, layer):
    batch, seq, _ = k.shape
    grid_rows = seq // GRID_W
    nblk = grid_rows // NA_Q_ROWS
    nbuf = NA_LOOKAHEAD + 1
    assert NA_HEADS % nbuf == 0
    q_block = (1, NA_HEADS, HEAD_PAD, NA_Q_TILE)
    return pl.pallas_call(
        functools.partial(_na_kernel, grid_rows=grid_rows),
        grid=(batch, nblk),
        in_specs=[
            pl.BlockSpec(q_block, lambda b, i: (b, 0, 0, i)),
            pl.BlockSpec(q_block, lambda b, i: (b, 0, 0, jnp.minimum(i + 1, nblk - 1))),
            pl.BlockSpec((1, seq, NA_WIDTH), lambda b, i: (b, 0, 0)),
            pl.BlockSpec((1, seq // NA_Q_TILE, NA_HEADS * V_ROWS, NA_Q_TILE), lambda b, i: (b, 0, 0, 0)),
            _const_spec((None, 3, NA_HEADS, NA_KEY_TILE, NA_Q_TILE), lambda b, i: (layer, 0, 0, 0, 0)),
        ],
        out_specs=pl.BlockSpec((1, NA_Q_TILE, NA_WIDTH), lambda b, i: (b, i, 0)),
        out_shape=jax.ShapeDtypeStruct((batch, seq, NA_WIDTH), BF16),
        scratch_shapes=[pltpu.VMEM((nbuf, NA_KEY_TILE, NA_Q_TILE), F32), pltpu.VMEM((NA_HEADS, NA_Q_TILE), F32),
                        pltpu.VMEM((NA_WIDTH, NA_Q_TILE), F32)],
        compiler_params=pltpu.CompilerParams(
            dimension_semantics=("parallel", "arbitrary"), vmem_limit_bytes=VMEM_LIMIT_BYTES),
        name="na_attention",
    )(q_t, q_t, k, v_t, bias)


def _na_bias_tables(rpb, grid_rows):
    depth = rpb.shape[0]
    nblk = grid_rows // NA_Q_ROWS
    n_col_off = 2 * NA_WIN_COLS - 1
    kc = np.arange(GRID_W)[:, None]
    c = np.arange(GRID_W)[None, :]
    cs = np.clip(c - NA_WIN_COLS // 2, 0, GRID_W - NA_WIN_COLS)
    col_ok = (kc >= cs) & (kc < cs + NA_WIN_COLS)
    col_off = kc - c + (NA_WIN_COLS - 1)
    select = ((col_off[None] == np.arange(n_col_off)[:, None, None]) & col_ok[None])
    select = select.reshape(n_col_off, GRID_W * GRID_W).astype(np.float32)
    col_tiles = jnp.dot((rpb * LOG2E).reshape(-1, n_col_off), select, precision=lax.Precision.HIGHEST)
    col_tiles = col_tiles.reshape(depth, NA_HEADS, 2 * NA_WIN_ROWS - 1, GRID_W, GRID_W)
    col_tiles = jnp.where(col_ok, col_tiles, MASK_VALUE)
    masked = jnp.full((depth, NA_HEADS, GRID_W, GRID_W), MASK_VALUE, F32)
    tables = []
    for i in (0, 1, nblk - 1):
        key_row0 = min(max(NA_Q_ROWS * i - NA_WIN_ROWS // 2, 0), grid_rows - NA_KEY_ROWS)
        rows_k = []
        for kr_rel in range(NA_KEY_ROWS):
            kr = key_row0 + kr_rel
            tiles = []
            for j in range(NA_Q_ROWS):
                r = NA_Q_ROWS * i + j
                rs = min(max(r - NA_WIN_ROWS // 2, 0), grid_rows - NA_WIN_ROWS)
                inside = rs <= kr < rs + NA_WIN_ROWS
                tiles.append(col_tiles[:, :, kr - r + NA_WIN_ROWS - 1] if inside else masked)
            rows_k.append(jnp.stack(tiles, axis=3))
        b = jnp.stack(rows_k, axis=2)
        tables.append(b.reshape(depth, NA_HEADS, NA_KEY_TILE, NA_Q_TILE))
    return jnp.stack(tables, axis=1)


def _mla_kernel(q_ref, qnext_ref, k_ref, v_ref, o_ref, s_ref, m_ref, ot_ref):
    tq = q_ref.shape[-1]
    nk = v_ref.shape[2]
    tk = v_ref.shape[-1]
    sub = 8
    nbuf = s_ref.shape[0]
    lookahead = nbuf - 1

    def score_tile(src_ref, hd, j, m_sub):
        s = jnp.dot(k_ref[0, hd, j * tk:(j + 1) * tk, :], src_ref[0, hd], preferred_element_type=F32)
        s_ref[hd % nbuf, j * tk:(j + 1) * tk, :] = s
        m_j = jnp.max(s.reshape(tk // sub, sub, tq), axis=0)
        return m_j if m_sub is None else jnp.maximum(m_sub, m_j)

    def value_tile(hd, j, m, acc):
        p = jnp.exp2(s_ref[hd % nbuf, j * tk:(j + 1) * tk, :] - m).astype(BF16)
        return acc + jnp.dot(v_ref[0, hd, j], p, preferred_element_type=F32)

    @pl.when(pl.program_id(1) == 0)
    def _first_tile_of_batch():
        for hd in range(lookahead):
            m_sub = None
            for j in range(nk):
                m_sub = score_tile(q_ref, hd, j, m_sub)
            m_ref[hd:hd + 1, :] = jnp.max(m_sub, axis=0, keepdims=True)

    maxima = {hd: m_ref[hd:hd + 1, :] for hd in range(lookahead)}
    for hd in range(MLA_HEADS):
        ahead = hd + lookahead
        src_ref, ahead_hd = (q_ref, ahead) if ahead < MLA_HEADS else (qnext_ref, ahead - MLA_HEADS)
        m_sub = None
        acc = jnp.zeros((V_ROWS, tq), F32)
        for j in range(nk):
            m_sub = score_tile(src_ref, ahead_hd, j, m_sub)
            acc = value_tile(hd, j, maxima[hd], acc)
        m_ahead = jnp.max(m_sub, axis=0, keepdims=True)
        if ahead < MLA_HEADS:
            maxima[ahead] = m_ahead
        else:
            m_ref[ahead_hd:ahead_hd + 1, :] = m_ahead
        ot_ref[hd * MLA_V_DIM:(hd + 1) * MLA_V_DIM, :] = (
            acc[:MLA_V_DIM] * (1.0 / acc[MLA_V_DIM:MLA_V_DIM + 1]))
    o_ref[0] = ot_ref[...].T.astype(BF16)


def _mla_attention(q_t, k, v_t):
    batch, _, seq, _ = k.shape
    nk, tk = v_t.shape[2], v_t.shape[4]
    tq = MLA_Q_TILE
    nq = seq // tq
    nbuf = MLA_LOOKAHEAD + 1
    assert MLA_HEADS % nbuf == 0
    q_block = (1, MLA_HEADS, HEAD_PAD, tq)
    return pl.pallas_call(
        _mla_kernel,
        grid=(batch, nq),
        in_specs=[
            pl.BlockSpec(q_block, lambda b, i: (b, 0, 0, i)),
            pl.BlockSpec(q_block, lambda b, i: (b, 0, 0, jnp.minimum(i + 1, nq - 1))),
            pl.BlockSpec((1, MLA_HEADS, seq, HEAD_PAD), lambda b, i: (b, 0, 0, 0)),
            pl.BlockSpec((1, MLA_HEADS, nk, V_ROWS, tk), lambda b, i: (b, 0, 0, 0, 0)),
        ],
        out_specs=pl.BlockSpec((1, tq, MLA_WIDTH), lambda b, i: (b, i, 0)),
        out_shape=jax.ShapeDtypeStruct((batch, seq, MLA_WIDTH), BF16),
        scratch_shapes=[pltpu.VMEM((nbuf, seq, tq), F32), pltpu.VMEM((MLA_HEADS, tq), F32),
                        pltpu.VMEM((MLA_WIDTH, tq), F32)],
        compiler_params=pltpu.CompilerParams(
            dimension_semantics=("parallel", "arbitrary"), vmem_limit_bytes=VMEM_LIMIT_BYTES),
        name="mla_attention",
    )(q_t, q_t, k, v_t)


def _post_kernel(x_ref, ona_ref, omla_ref, gates_ref, mod_ref, wbn_ref, wbm_ref, wout_ref, gmlp_ref,
                 w1_ref, w2_ref, gfin_ref, o_ref, *, final):
    d = x_ref.shape[-1]
    x = x_ref[0]
    mod = mod_ref[0]
    gate_a, shift_m, scale_m, gate_m = mod[2:3], mod[3:4], mod[4:5], mod[5:6]
    gates = gates_ref[0]
    br_na = jnp.dot(ona_ref[0], wbn_ref[...], preferred_element_type=F32)
    br_mla = jnp.dot(omla_ref[0], wbm_ref[...], preferred_element_type=F32)
    merged = gates[:, :d].astype(F32) * br_na + gates[:, d:].astype(F32) * br_mla
    x = x + gate_a * jnp.dot(merged.astype(BF16), wout_ref[...], preferred_element_type=F32)

    h = (_rms(x, gmlp_ref[...]) * (1.0 + scale_m) + shift_m).astype(BF16)
    u = jnp.dot(h, w1_ref[...], preferred_element_type=F32)
    u = jnp.square(jnp.maximum(u, 0.0)).astype(BF16)
    x = x + gate_m * jnp.dot(u, w2_ref[...], preferred_element_type=F32)
    if final:
        x = _rms(x, gfin_ref[...])
    o_ref[0] = x


def _post(x, o_na, o_mla, gates, mod_l, layer, p, g_final, final):
    batch, seq, d = x.shape
    d_ff = p["w1"].shape[-1]
    tm = POST_TILE
    tok = lambda b, t: (b, t, 0)
    w3 = lambda b, t: (layer, 0, 0)
    return pl.pallas_call(
        functools.partial(_post_kernel, final=final),
        grid=(batch, seq // tm),
        in_specs=[
            pl.BlockSpec((1, tm, d), tok),
            pl.BlockSpec((1, tm, NA_WIDTH), tok),
            pl.BlockSpec((1, tm, MLA_WIDTH), tok),
            pl.BlockSpec((1, tm, 2 * d), tok),
            pl.BlockSpec((1, 6, d), lambda b, t: (b, 0, 0)),
            _const_spec((None, NA_WIDTH, d), w3),
            _const_spec((None, MLA_WIDTH, d), w3),
            _const_spec((None, d, d), w3),
            _const_spec((None, 1, d), w3),
            _const_spec((None, d, d_ff), w3),
            _const_spec((None, d_ff, d), w3),
            _const_spec((1, d), lambda b, t: (0, 0)),
        ],
        out_specs=pl.BlockSpec((1, tm, d), tok),
        out_shape=jax.ShapeDtypeStruct((batch, seq, d), F32),
        compiler_params=pltpu.CompilerParams(
            dimension_semantics=("parallel", "parallel"), vmem_limit_bytes=VMEM_LIMIT_BYTES),
        name="post",
    )(x, o_na, o_mla, gates, mod_l, p["w_br_na"], p["w_br_mla"], p["w_out"], p["g_mlp"], p["w1"], p["w2"],
      g_final)


def _prepare_params(g_mix, w_in, b_gate, g_q, w_uq, g_kv, w_ukv, w_br_na, w_br_mla, w_out, g_mlp, w_ff1, w_ff2):
    depth, d, _ = w_in.shape
    o = 0
    cols = {}
    for name, size in (("q_na", NA_WIDTH), ("k_na", NA_WIDTH), ("v_na", NA_WIDTH), ("c_q", MLA_Q_RANK),
                       ("c_kv", MLA_KV_RANK), ("k_rope", MLA_ROPE_DIM), ("gate", 2 * d)):
        cols[name] = w_in[:, :, o:o + size]
        o += size
    half = MLA_ROPE_DIM // 2
    rope_pad = jnp.zeros((depth, d, LANES - MLA_ROPE_DIM), F32)
    kr = cols["k_rope"]
    kr_swapped = jnp.concatenate([kr[:, :, half:], kr[:, :, :half]], axis=-1)
    wc = jnp.concatenate([cols["c_q"], cols["c_kv"], kr, rope_pad, kr_swapped, rope_pad], axis=-1)

    qk_dim = MLA_NOPE_DIM + MLA_ROPE_DIM
    wuq = w_uq.reshape(depth, MLA_Q_RANK, MLA_HEADS, qk_dim)
    wuq = jnp.pad(wuq, ((0, 0), (0, 0), (0, 0), (0, HEAD_PAD - qk_dim)))
    wuq_t = wuq.reshape(depth, MLA_Q_RANK, MLA_HEADS * HEAD_PAD).transpose(0, 2, 1)

    wukv = w_ukv.reshape(depth, MLA_KV_RANK, MLA_HEADS, MLA_NOPE_DIM + MLA_V_DIM)
    wuk = jnp.pad(wukv[..., :MLA_NOPE_DIM], ((0, 0), (0, 0), (0, 0), (0, HEAD_PAD - MLA_NOPE_DIM)))
    wuk = wuk.reshape(depth, MLA_KV_RANK, MLA_HEADS * HEAD_PAD)
    place = jnp.zeros((LANES, MLA_HEADS, HEAD_PAD), F32)
    j = jnp.arange(MLA_ROPE_DIM)
    place = place.at[j, :, MLA_NOPE_DIM + j].set(1.0).reshape(LANES, MLA_HEADS * HEAD_PAD)
    wkk = jnp.concatenate([wuk, jnp.broadcast_to(place, (depth,) + place.shape)], axis=1)
    wuv_t = wukv[..., MLA_NOPE_DIM:].reshape(depth, MLA_KV_RANK, MLA_WIDTH).transpose(0, 2, 1)

    return {
        "g_mix": g_mix.reshape(depth, 1, d),
        "wqv_t": jnp.concatenate([cols["q_na"], cols["v_na"]], axis=-1).transpose(0, 2, 1).astype(BF16),
        "wk": cols["k_na"].astype(BF16),
        "wc": wc.astype(BF16),
        "wg": cols["gate"].astype(BF16),
        "b_gate": b_gate.reshape(depth, 1, 2 * d),
        "g_q": g_q.reshape(depth, 1, MLA_Q_RANK),
        "g_kv": g_kv.reshape(depth, 1, MLA_KV_RANK),
        "wuq_t": wuq_t.astype(BF16),
        "wkk": wkk.astype(BF16),
        "wuv_t": wuv_t.astype(BF16),
        "w_br_na": w_br_na.astype(BF16),
        "w_br_mla": w_br_mla.astype(BF16),
        "w_out": w_out.astype(BF16),
        "g_mlp": g_mlp.reshape(depth, 1, d),
        "w1": w_ff1.astype(BF16),
        "w2": w_ff2.astype(BF16),
    }


def _rope_tables(positions):
    half = MLA_ROPE_DIM // 2
    inv_freq = 1.0 / (ROPE_THETA ** (jnp.arange(0, MLA_ROPE_DIM, 2, dtype=F32) / MLA_ROPE_DIM))
    ang = positions.astype(F32)[..., None] * inv_freq
    cos, sin = jnp.cos(ang), jnp.sin(ang)
    pad = jnp.zeros(ang.shape[:-1] + (LANES - 2 * half,), F32)
    cos2 = jnp.concatenate([cos, cos, pad], axis=-1)
    sin2 = jnp.concatenate([-sin, sin, pad], axis=-1)
    return cos.transpose(0, 2, 1), sin.transpose(0, 2, 1), cos2, sin2


def kernel(x, c, positions, w_ada, b_ada, g_mix, w_in, b_gate, rpb, g_q, w_uq, g_kv, w_ukv, w_br_na, w_br_mla,
           w_out, g_mlp, w_ff1, w_ff2, g_final):
    batch, seq, d = x.shape
    depth = w_in.shape[0]
    grid_rows = seq // GRID_W
    assert seq % TOKEN_TILE == 0 and grid_rows % NA_Q_ROWS == 0 and grid_rows >= NA_KEY_ROWS + NA_Q_ROWS

    p = _prepare_params(g_mix, w_in, b_gate, g_q, w_uq, g_kv, w_ukv, w_br_na, w_br_mla, w_out, g_mlp,
                        w_ff1, w_ff2)
    rope = _rope_tables(positions)
    bias = _na_bias_tables(rpb, grid_rows)
    mod = _ada_mod(c, w_ada, b_ada).reshape(depth, batch, 6, d)
    g_fin = g_final.reshape(1, d)

    for layer in range(depth):
        q_na, k_na, v_na, q_m, k_m, v_m, gates = _projections(x, mod[layer], layer, p, rope)
        o_na = _na_attention(q_na, k_na, v_na, bias, layer)
        o_mla = _mla_attention(q_m, k_m, v_m)
        x = _post(x, o_na, o_mla, gates, mod[layer], layer, p, g_fin, layer == depth - 1)
    return x
```

```python
import functools
import math

import numpy as np
import jax
import jax.numpy as jnp
from jax import lax
from jax.experimental import pallas as pl
from jax.experimental.pallas import tpu as pltpu

F32 = jnp.float32
BF16 = jnp.bfloat16

GRID_W = 64
NA_HEADS = 8
NA_HEAD_DIM = 64
NA_WIN_ROWS = 8
NA_WIN_COLS = 16
MLA_HEADS = 8
MLA_Q_RANK = 256
MLA_KV_RANK = 128
MLA_NOPE_DIM = 64
MLA_ROPE_DIM = 32
MLA_V_DIM = 64
ROPE_THETA = 10000.0
NORM_EPS = 1e-6
NA_WIDTH = NA_HEADS * NA_HEAD_DIM
MLA_WIDTH = MLA_HEADS * MLA_V_DIM

LANES = 128
HEAD_PAD = 128
BF16_SUBLANES = 16
V_ROWS = NA_HEAD_DIM + BF16_SUBLANES
LOG2E = math.log2(math.e)
VMEM_LIMIT_BYTES = 56 * 1024 * 1024

TOKEN_TILE = 512
POST_TILE = 512
POST_SPLIT = 2
POST_FF_CHUNK = 2048
MLA_Q_TILE = 256
MLA_LOOKAHEAD = 3
NA_Q_ROWS = 4
NA_KEY_ROWS = 12
NA_Q_TILE = NA_Q_ROWS * GRID_W
NA_KEY_TILE = NA_KEY_ROWS * GRID_W
NA_LOOKAHEAD = 3
MASK_VALUE = -1e30


def _const_spec(shape, index_map):
    return pl.BlockSpec(shape, index_map, pipeline_mode=pl.Buffered(1))


def _rms(x, gain):
    return x * lax.rsqrt(jnp.mean(x * x, axis=-1, keepdims=True) + NORM_EPS) * gain


def _nt_dot(a, b):
    return lax.dot_general(a, b, (((1,), (1,)), ((), ())), preferred_element_type=F32)


def _ada_kernel(c_ref, w_ref, b_ref, o_ref):
    c = c_ref[...]
    c_act = (c / (1.0 + jnp.exp(-c))).astype(BF16)
    o_ref[0] = jnp.dot(c_act, w_ref[0].astype(BF16), preferred_element_type=F32) + b_ref[0]


def _ada_mod(c, w_ada, b_ada):
    depth, d, d6 = w_ada.shape
    batch = c.shape[0]
    nchunk = d6 // d
    return pl.pallas_call(
        _ada_kernel,
        grid=(depth, nchunk),
        in_specs=[
            pl.BlockSpec((batch, d), lambda l, j: (0, 0)),
            pl.BlockSpec((1, d, d), lambda l, j: (l, 0, j)),
            pl.BlockSpec((1, 1, d), lambda l, j: (l, 0, j)),
        ],
        out_specs=pl.BlockSpec((1, batch, d), lambda l, j: (l, 0, j)),
        out_shape=jax.ShapeDtypeStruct((depth, batch, d6), F32),
        name="ada_mod",
    )(c, w_ada, b_ada.reshape(depth, 1, d6))


def _proj_kernel(x_ref, mod_ref, gmix_ref, wqv_ref, wk_ref, wc_ref, wg_ref, bg_ref, gq_ref, gkv_ref,
                 wuq_ref, wkk_ref, wuv_ref, cost_ref, sint_ref, cos2_ref, sin2_ref,
                 qna_ref, kna_ref, vna_ref, qm_ref, km_ref, vm_ref, gates_ref):
    tm = x_ref.shape[1]
    x = x_ref[0]
    mod = mod_ref[0]
    shift, scale = mod[0:1], mod[1:2]
    h = (_rms(x, gmix_ref[...]) * (1.0 + scale) + shift).astype(BF16)

    qv_t = _nt_dot(wqv_ref[...], h)
    q_t = (qv_t[:NA_WIDTH] * (NA_HEAD_DIM ** -0.5 * LOG2E)).astype(BF16)
    zeros = jnp.zeros((NA_HEAD_DIM, tm), BF16)
    ones = jnp.ones((V_ROWS - NA_HEAD_DIM, NA_Q_TILE), BF16)
    v_t = qv_t[NA_WIDTH:].astype(BF16)
    for hd in range(NA_HEADS):
        q_h = q_t[hd * NA_HEAD_DIM:(hd + 1) * NA_HEAD_DIM]
        lo, hi = (q_h, zeros) if hd % 2 == 0 else (zeros, q_h)
        qna_ref[0, hd, :NA_HEAD_DIM, :] = lo
        qna_ref[0, hd, NA_HEAD_DIM:, :] = hi
        for ch in range(tm // NA_Q_TILE):
            vna_ref[0, ch, hd * V_ROWS:hd * V_ROWS + NA_HEAD_DIM, :] = (
                v_t[hd * NA_HEAD_DIM:(hd + 1) * NA_HEAD_DIM, ch * NA_Q_TILE:(ch + 1) * NA_Q_TILE])
            vna_ref[0, ch, hd * V_ROWS + NA_HEAD_DIM:(hd + 1) * V_ROWS, :] = ones
    kna_ref[0] = jnp.dot(h, wk_ref[...], preferred_element_type=F32).astype(BF16)

    logits = jnp.dot(h, wg_ref[...], preferred_element_type=F32) + bg_ref[...]
    gates_ref[0] = (1.0 / (1.0 + jnp.exp(-logits))).astype(BF16)

    lat = jnp.dot(h, wc_ref[...], preferred_element_type=F32)
    c_q = lat[:, :MLA_Q_RANK]
    c_kv = lat[:, MLA_Q_RANK:MLA_Q_RANK + MLA_KV_RANK]
    kr = lat[:, MLA_Q_RANK + MLA_KV_RANK:MLA_Q_RANK + MLA_KV_RANK + LANES]
    kr_swapped = lat[:, MLA_Q_RANK + MLA_KV_RANK + LANES:]
    cq_n = _rms(c_q, gq_ref[...]).astype(BF16)
    ckv_n = _rms(c_kv, gkv_ref[...]).astype(BF16)

    q3 = _nt_dot(wuq_ref[...], cq_n).reshape(MLA_HEADS, HEAD_PAD, tm)
    cos_t, sin_t = cost_ref[0], sint_ref[0]
    half = MLA_ROPE_DIM // 2
    x1 = q3[:, MLA_NOPE_DIM:MLA_NOPE_DIM + half]
    x2 = q3[:, MLA_NOPE_DIM + half:MLA_NOPE_DIM + MLA_ROPE_DIM]
    q3 = jnp.concatenate(
        [q3[:, :MLA_NOPE_DIM], x1 * cos_t - x2 * sin_t, x2 * cos_t + x1 * sin_t,
         q3[:, MLA_NOPE_DIM + MLA_ROPE_DIM:]], axis=1)
    qm_ref[0] = (q3 * ((MLA_NOPE_DIM + MLA_ROPE_DIM) ** -0.5 * LOG2E)).astype(BF16)

    k_rot = (kr * cos2_ref[0] + kr_swapped * sin2_ref[0]).astype(BF16)
    k_all = jnp.dot(jnp.concatenate([ckv_n, k_rot], axis=1), wkk_ref[...],
                    preferred_element_type=F32).astype(BF16)
    for hd in range(MLA_HEADS):
        km_ref[0, hd] = k_all[:, hd * HEAD_PAD:(hd + 1) * HEAD_PAD]
    vm_ref[0, :, 0, :MLA_V_DIM, :] = _nt_dot(wuv_ref[...], ckv_n).astype(BF16).reshape(MLA_HEADS, MLA_V_DIM, tm)
    vm_ref[0, :, 0, MLA_V_DIM:, :] = jnp.ones((MLA_HEADS, V_ROWS - MLA_V_DIM, tm), BF16)


def _projections(x, mod_l, layer, p, rope):
    batch, seq, d = x.shape
    tm = TOKEN_TILE
    nt = seq // tm
    cos_t, sin_t, cos2, sin2 = rope
    lat_cols = p["wc"].shape[-1]
    tok = lambda b, t: (b, t, 0)
    per_batch = lambda b, t: (b, 0, 0)
    w3 = lambda b, t: (layer, 0, 0)
    out_shapes = (
        jax.ShapeDtypeStruct((batch, NA_HEADS, HEAD_PAD, seq), BF16),
        jax.ShapeDtypeStruct((batch, seq, NA_WIDTH), BF16),
        jax.ShapeDtypeStruct((batch, seq // NA_Q_TILE, NA_HEADS * V_ROWS, NA_Q_TILE), BF16),
        jax.ShapeDtypeStruct((batch, MLA_HEADS, HEAD_PAD, seq), BF16),
        jax.ShapeDtypeStruct((batch, MLA_HEADS, seq, HEAD_PAD), BF16),
        jax.ShapeDtypeStruct((batch, MLA_HEADS, nt, V_ROWS, tm), BF16),
        jax.ShapeDtypeStruct((batch, seq, 2 * d), BF16),
    )
    out_specs = (
        pl.BlockSpec((1, NA_HEADS, HEAD_PAD, tm), lambda b, t: (b, 0, 0, t)),
        pl.BlockSpec((1, tm, NA_WIDTH), tok),
        pl.BlockSpec((1, tm // NA_Q_TILE, NA_HEADS * V_ROWS, NA_Q_TILE), lambda b, t: (b, t, 0, 0)),
        pl.BlockSpec((1, MLA_HEADS, HEAD_PAD, tm), lambda b, t: (b, 0, 0, t)),
        pl.BlockSpec((1, MLA_HEADS, tm, HEAD_PAD), lambda b, t: (b, 0, t, 0)),
        pl.BlockSpec((1, MLA_HEADS, 1, V_ROWS, tm), lambda b, t: (b, 0, t, 0, 0)),
        pl.BlockSpec((1, tm, 2 * d), tok),
    )
    in_specs = [
        pl.BlockSpec((1, tm, d), tok),
        pl.BlockSpec((1, 6, d), per_batch),
        _const_spec((None, 1, d), w3),
        _const_spec((None, 2 * NA_WIDTH, d), w3),
        _const_spec((None, d, NA_WIDTH), w3),
        _const_spec((None, d, lat_cols), w3),
        _const_spec((None, d, 2 * d), w3),
        _const_spec((None, 1, 2 * d), w3),
        _const_spec((None, 1, MLA_Q_RANK), w3),
        _const_spec((None, 1, MLA_KV_RANK), w3),
        _const_spec((None, MLA_HEADS * HEAD_PAD, MLA_Q_RANK), w3),
        _const_spec((None, MLA_KV_RANK + LANES, MLA_HEADS * HEAD_PAD), w3),
        _const_spec((None, MLA_WIDTH, MLA_KV_RANK), w3),
        pl.BlockSpec((1, MLA_ROPE_DIM // 2, tm), lambda b, t: (b, 0, t)),
        pl.BlockSpec((1, MLA_ROPE_DIM // 2, tm), lambda b, t: (b, 0, t)),
        pl.BlockSpec((1, tm, LANES), tok),
        pl.BlockSpec((1, tm, LANES), tok),
    ]
    return pl.pallas_call(
        _proj_kernel,
        grid=(batch, nt),
        in_specs=in_specs,
        out_specs=out_specs,
        out_shape=out_shapes,
        compiler_params=pltpu.CompilerParams(
            dimension_semantics=("parallel", "parallel"), vmem_limit_bytes=VMEM_LIMIT_BYTES),
        name="projections",
    )(x, mod_l, p["g_mix"], p["wqv_t"], p["wk"], p["wc"], p["wg"], p["b_gate"], p["g_q"], p["g_kv"],
      p["wuq_t"], p["wkk"], p["wuv_t"], cos_t, sin_t, cos2, sin2)


def _na_kernel(q_ref, qnext_ref, k_ref, v_ref, bias_ref, o_ref, s_ref, m_ref, ot_ref, *, grid_rows):
    i = pl.program_id(1)
    nblk = pl.num_programs(1)
    nchunk = NA_KEY_TILE // NA_Q_TILE
    sub = 8
    nbuf = s_ref.shape[0]
    lookahead = nbuf - 1

    def window(step):
        key_row0 = jnp.clip(NA_Q_ROWS * step - NA_WIN_ROWS // 2, 0, grid_rows - NA_KEY_ROWS)
        pattern = jnp.where(step == 0, 0, jnp.where(step == nblk - 1, 2, 1))
        return pl.multiple_of(key_row0 * GRID_W, NA_Q_TILE), key_row0 // NA_Q_ROWS, pattern

    def scores(src_ref, win, hd):
        key0, _, pattern = win
        pair = hd // 2
        k_pair = k_ref[0, pl.ds(key0, NA_KEY_TILE), pair * LANES:(pair + 1) * LANES]
        s = jnp.dot(k_pair, src_ref[0, hd], preferred_element_type=F32)
        s = s + bias_ref[pattern, hd]
        s_ref[hd % nbuf] = s
        m_sub = jnp.max(s.reshape(NA_KEY_TILE // sub, sub, NA_Q_TILE), axis=0)
        return jnp.max(m_sub, axis=0, keepdims=True)

    def weighted_values(win, hd, m):
        _, chunk0, _ = win
        o_t = jnp.zeros((V_ROWS, NA_Q_TILE), F32)
        for ch in range(nchunk):
            p = jnp.exp2(s_ref[hd % nbuf, ch * NA_Q_TILE:(ch + 1) * NA_Q_TILE, :] - m).astype(BF16)
            v_t = v_ref[0, chunk0 + ch, hd * V_ROWS:(hd + 1) * V_ROWS, :]
            o_t = o_t + jnp.dot(v_t, p, preferred_element_type=F32)
        denom = o_t[NA_HEAD_DIM:NA_HEAD_DIM + 1]
        ot_ref[hd * NA_HEAD_DIM:(hd + 1) * NA_HEAD_DIM, :] = o_t[:NA_HEAD_DIM] * (1.0 / denom)

    cur = window(i)
    nxt = window(jnp.minimum(i + 1, nblk - 1))

    @pl.when(i == 0)
    def _first_block_of_batch():
        for hd in range(lookahead):
            m_ref[hd:hd + 1, :] = scores(q_ref, cur, hd)

    maxima = {hd: m_ref[hd:hd + 1, :] for hd in range(lookahead)}
    for hd in range(NA_HEADS):
        ahead = hd + lookahead
        if ahead < NA_HEADS:
            maxima[ahead] = scores(q_ref, cur, ahead)
        else:
            m_ref[ahead - NA_HEADS:ahead - NA_HEADS + 1, :] = scores(qnext_ref, nxt, ahead - NA_HEADS)
        weighted_values(cur, hd, maxima[hd])
    o_ref[0] = ot_ref[...].T.astype(BF16)


def _na_attention(q_t, k, v_t, bias, layer):
    batch, seq, _ = k.shape
    grid_rows = seq // GRID_W
    nblk = grid_rows // NA_Q_ROWS
    nbuf = NA_LOOKAHEAD + 1
    assert NA_HEADS % nbuf == 0
    q_block = (1, NA_HEADS, HEAD_PAD, NA_Q_TILE)
    return pl.pallas_call(
        functools.partial(_na_kernel, grid_rows=grid_rows),
        grid=(batch, nblk),
        in_specs=[
            pl.BlockSpec(q_block, lambda b, i: (b, 0, 0, i)),
            pl.BlockSpec(q_block, lambda b, i: (b, 0, 0, jnp.minimum(i + 1, nblk - 1))),
            pl.BlockSpec((1, seq, NA_WIDTH), lambda b, i: (b, 0, 0)),
            pl.BlockSpec((1, seq // NA_Q_TILE, NA_HEADS * V_ROWS, NA_Q_TILE), lambda b, i: (b, 0, 0, 0)),
            _const_spec((None, 3, NA_HEADS, NA_KEY_TILE, NA_Q_TILE), lambda b, i: (layer, 0, 0, 0, 0)),
        ],
        out_specs=pl.BlockSpec((1, NA_Q_TILE, NA_WIDTH), lambda b, i: (b, i, 0)),
        out_shape=jax.ShapeDtypeStruct((batch, seq, NA_WIDTH), BF16),
        scratch_shapes=[pltpu.VMEM((nbuf, NA_KEY_TILE, NA_Q_TILE), F32), pltpu.VMEM((NA_HEADS, NA_Q_TILE), F32),
                        pltpu.VMEM((NA_WIDTH, NA_Q_TILE), F32)],
        compiler_params=pltpu.CompilerParams(
            dimension_semantics=("parallel", "arbitrary"), vmem_limit_bytes=VMEM_LIMIT_BYTES),
        name="na_attention",
    )(q_t, q_t, k, v_t, bias)


def _na_bias_tables(rpb, grid_rows):
    depth = rpb.shape[0]
    n_col_off = 2 * NA_WIN_COLS - 1
    kc = np.arange(GRID_W)[:, None]
    c = np.arange(GRID_W)[None, :]
    cs = np.clip(c - NA_WIN_COLS // 2, 0, GRID_W - NA_WIN_COLS)
    col_ok = (kc >= cs) & (kc < cs + NA_WIN_COLS)
    col_off = kc - c + (NA_WIN_COLS - 1)
    select = ((col_off[None] == np.arange(n_col_off)[:, None, None]) & col_ok[None])
    select = select.reshape(n_col_off, GRID_W * GRID_W).astype(np.float32)
    col_tiles = jnp.dot((rpb * LOG2E).reshape(-1, n_col_off), select, precision=lax.Precision.HIGHEST)
    col_tiles = col_tiles.reshape(depth, NA_HEADS, 2 * NA_WIN_ROWS - 1, GRID_W, GRID_W)
    col_tiles = jnp.where(col_ok, col_tiles, MASK_VALUE)
    return pl.pallas_call(
        functools.partial(_bias_kernel, grid_rows=grid_rows),
        grid=(depth, NA_HEADS),
        in_specs=[pl.BlockSpec((1, 1, 2 * NA_WIN_ROWS - 1, GRID_W, GRID_W), lambda l, h: (l, h, 0, 0, 0))],
        out_specs=pl.BlockSpec((1, 3, 1, NA_KEY_TILE, NA_Q_TILE), lambda l, h: (l, 0, h, 0, 0)),
        out_shape=jax.ShapeDtypeStruct((depth, 3, NA_HEADS, NA_KEY_TILE, NA_Q_TILE), F32),
        name="na_bias",
    )(col_tiles)


def _bias_kernel(col_ref, o_ref, *, grid_rows):
    nblk = grid_rows // NA_Q_ROWS
    masked = jnp.full((GRID_W, GRID_W), MASK_VALUE, F32)
    for pattern, i in enumerate((0, 1, nblk - 1)):
        key_row0 = min(max(NA_Q_ROWS * i - NA_WIN_ROWS // 2, 0), grid_rows - NA_KEY_ROWS)
        for kr_rel in range(NA_KEY_ROWS):
            kr = key_row0 + kr_rel
            tiles = []
            for j in range(NA_Q_ROWS):
                r = NA_Q_ROWS * i + j
                rs = min(max(r - NA_WIN_ROWS // 2, 0), grid_rows - NA_WIN_ROWS)
                inside = rs <= kr < rs + NA_WIN_ROWS
                tiles.append(col_ref[0, 0, kr - r + NA_WIN_ROWS - 1] if inside else masked)
            o_ref[0, pattern, 0, kr_rel * GRID_W:(kr_rel + 1) * GRID_W, :] = jnp.concatenate(tiles, axis=1)


def _mla_kernel(q_ref, qnext_ref, k_ref, v_ref, o_ref, s_ref, m_ref, ot_ref):
    tq = q_ref.shape[-1]
    nk = v_ref.shape[2]
    tk = v_ref.shape[-1]
    sub = 8
    nbuf = s_ref.shape[0]
    lookahead = nbuf - 1

    def score_tile(src_ref, hd, j, m_sub):
        s = jnp.dot(k_ref[0, hd, j * tk:(j + 1) * tk, :], src_ref[0, hd], preferred_element_type=F32)
        s_ref[hd % nbuf, j * tk:(j + 1) * tk, :] = s
        m_j = jnp.max(s.reshape(tk // sub, sub, tq), axis=0)
        return m_j if m_sub is None else jnp.maximum(m_sub, m_j)

    def value_tile(hd, j, m, acc):
        p = jnp.exp2(s_ref[hd % nbuf, j * tk:(j + 1) * tk, :] - m).astype(BF16)
        return acc + jnp.dot(v_ref[0, hd, j], p, preferred_element_type=F32)

    @pl.when(pl.program_id(1) == 0)
    def _first_tile_of_batch():
        for hd in range(lookahead):
            m_sub = None
            for j in range(nk):
                m_sub = score_tile(q_ref, hd, j, m_sub)
            m_ref[hd:hd + 1, :] = jnp.max(m_sub, axis=0, keepdims=True)

    maxima = {hd: m_ref[hd:hd + 1, :] for hd in range(lookahead)}
    for hd in range(MLA_HEADS):
        ahead = hd + lookahead
        src_ref, ahead_hd = (q_ref, ahead) if ahead < MLA_HEADS else (qnext_ref, ahead - MLA_HEADS)
        m_sub = None
        acc = jnp.zeros((V_ROWS, tq), F32)
        for j in range(nk):
            m_sub = score_tile(src_ref, ahead_hd, j, m_sub)
            acc = value_tile(hd, j, maxima[hd], acc)
        m_ahead = jnp.max(m_sub, axis=0, keepdims=True)
        if ahead < MLA_HEADS:
            maxima[ahead] = m_ahead
        else:
            m_ref[ahead_hd:ahead_hd + 1, :] = m_ahead
        ot_ref[hd * MLA_V_DIM:(hd + 1) * MLA_V_DIM, :] = (
            acc[:MLA_V_DIM] * (1.0 / acc[MLA_V_DIM:MLA_V_DIM + 1]))
    o_ref[0] = ot_ref[...].T.astype(BF16)


def _mla_attention(q_t, k, v_t):
    batch, _, seq, _ = k.shape
    nk, tk = v_t.shape[2], v_t.shape[4]
    tq = MLA_Q_TILE
    nq = seq // tq
    nbuf = MLA_LOOKAHEAD + 1
    assert MLA_HEADS % nbuf == 0
    q_block = (1, MLA_HEADS, HEAD_PAD, tq)
    return pl.pallas_call(
        _mla_kernel,
        grid=(batch, nq),
        in_specs=[
            pl.BlockSpec(q_block, lambda b, i: (b, 0, 0, i)),
            pl.BlockSpec(q_block, lambda b, i: (b, 0, 0, jnp.minimum(i + 1, nq - 1))),
            pl.BlockSpec((1, MLA_HEADS, seq, HEAD_PAD), lambda b, i: (b, 0, 0, 0)),
            pl.BlockSpec((1, MLA_HEADS, nk, V_ROWS, tk), lambda b, i: (b, 0, 0, 0, 0)),
        ],
        out_specs=pl.BlockSpec((1, tq, MLA_WIDTH), lambda b, i: (b, i, 0)),
        out_shape=jax.ShapeDtypeStruct((batch, seq, MLA_WIDTH), BF16),
        scratch_shapes=[pltpu.VMEM((nbuf, seq, tq), F32), pltpu.VMEM((MLA_HEADS, tq), F32),
                        pltpu.VMEM((MLA_WIDTH, tq), F32)],
        compiler_params=pltpu.CompilerParams(
            dimension_semantics=("parallel", "arbitrary"), vmem_limit_bytes=VMEM_LIMIT_BYTES),
        name="mla_attention",
    )(q_t, q_t, k, v_t)


def _post_kernel(x_ref, ona_ref, omla_ref, gates_ref, mod_ref, wbn_ref, wbm_ref, wout_ref, gmlp_ref,
                 w1_ref, w2_ref, gfin_ref, o_ref, *, final):
    tm, d = x_ref.shape[1], x_ref.shape[2]
    d_ff = w1_ref.shape[-1]
    mod = mod_ref[0]
    gate_a, shift_m, scale_m, gate_m = mod[2:3], mod[3:4], mod[4:5], mod[5:6]

    def mix_tokens(rows):
        gates = gates_ref[0, rows, :]
        br_na = jnp.dot(ona_ref[0, rows, :], wbn_ref[...], preferred_element_type=F32)
        br_mla = jnp.dot(omla_ref[0, rows, :], wbm_ref[...], preferred_element_type=F32)
        merged = gates[:, :d].astype(F32) * br_na + gates[:, d:].astype(F32) * br_mla
        x = x_ref[0, rows, :] + gate_a * jnp.dot(merged.astype(BF16), wout_ref[...], preferred_element_type=F32)
        h = (_rms(x, gmlp_ref[...]) * (1.0 + scale_m) + shift_m).astype(BF16)
        return x, h

    def mix_channels(x, h):
        y = jnp.zeros_like(x)
        for c0 in range(0, d_ff, POST_FF_CHUNK):
            u = jnp.dot(h, w1_ref[:, c0:c0 + POST_FF_CHUNK], preferred_element_type=F32)
            u = jnp.square(jnp.maximum(u, 0.0)).astype(BF16)
            y = y + jnp.dot(u, w2_ref[c0:c0 + POST_FF_CHUNK, :], preferred_element_type=F32)
        x = x + gate_m * y
        return _rms(x, gfin_ref[...]) if final else x

    rows = [pl.ds(r0, tm // POST_SPLIT) for r0 in range(0, tm, tm // POST_SPLIT)]
    mixed = [mix_tokens(r) for r in rows]
    for r, (x, h) in zip(rows, mixed):
        o_ref[0, r, :] = mix_channels(x, h)


def _post(x, o_na, o_mla, gates, mod_l, layer, p, g_final, final):
    batch, seq, d = x.shape
    d_ff = p["w1"].shape[-1]
    tm = POST_TILE
    tok = lambda b, t: (b, t, 0)
    w3 = lambda b, t: (layer, 0, 0)
    return pl.pallas_call(
        functools.partial(_post_kernel, final=final),
        grid=(batch, seq // tm),
        in_specs=[
            pl.BlockSpec((1, tm, d), tok),
            pl.BlockSpec((1, tm, NA_WIDTH), tok),
            pl.BlockSpec((1, tm, MLA_WIDTH), tok),
            pl.BlockSpec((1, tm, 2 * d), tok),
            pl.BlockSpec((1, 6, d), lambda b, t: (b, 0, 0)),
            _const_spec((None, NA_WIDTH, d), w3),
            _const_spec((None, MLA_WIDTH, d), w3),
            _const_spec((None, d, d), w3),
            _const_spec((None, 1, d), w3),
            _const_spec((None, d, d_ff), w3),
            _const_spec((None, d_ff, d), w3),
            _const_spec((1, d), lambda b, t: (0, 0)),
        ],
        out_specs=pl.BlockSpec((1, tm, d), tok),
        out_shape=jax.ShapeDtypeStruct((batch, seq, d), F32),
        compiler_params=pltpu.CompilerParams(
            dimension_semantics=("parallel", "parallel"), vmem_limit_bytes=VMEM_LIMIT_BYTES),
        name="post",
    )(x, o_na, o_mla, gates, mod_l, p["w_br_na"], p["w_br_mla"], p["w_out"], p["g_mlp"], p["w1"], p["w2"],
      g_final)


def _prepare_params(g_mix, w_in, b_gate, g_q, w_uq, g_kv, w_ukv, w_br_na, w_br_mla, w_out, g_mlp, w_ff1, w_ff2):
    depth, d, _ = w_in.shape
    o = 0
    cols = {}
    for name, size in (("q_na", NA_WIDTH), ("k_na", NA_WIDTH), ("v_na", NA_WIDTH), ("c_q", MLA_Q_RANK),
                       ("c_kv", MLA_KV_RANK), ("k_rope", MLA_ROPE_DIM), ("gate", 2 * d)):
        cols[name] = w_in[:, :, o:o + size]
        o += size
    half = MLA_ROPE_DIM // 2
    rope_pad = jnp.zeros((depth, d, LANES - MLA_ROPE_DIM), F32)
    kr = cols["k_rope"]
    kr_swapped = jnp.concatenate([kr[:, :, half:], kr[:, :, :half]], axis=-1)
    wc = jnp.concatenate([cols["c_q"], cols["c_kv"], kr, rope_pad, kr_swapped, rope_pad], axis=-1)

    qk_dim = MLA_NOPE_DIM + MLA_ROPE_DIM
    wuq = w_uq.reshape(depth, MLA_Q_RANK, MLA_HEADS, qk_dim)
    wuq = jnp.pad(wuq, ((0, 0), (0, 0), (0, 0), (0, HEAD_PAD - qk_dim)))
    wuq_t = wuq.reshape(depth, MLA_Q_RANK, MLA_HEADS * HEAD_PAD).transpose(0, 2, 1)

    wukv = w_ukv.reshape(depth, MLA_KV_RANK, MLA_HEADS, MLA_NOPE_DIM + MLA_V_DIM)
    wuk = jnp.pad(wukv[..., :MLA_NOPE_DIM], ((0, 0), (0, 0), (0, 0), (0, HEAD_PAD - MLA_NOPE_DIM)))
    wuk = wuk.reshape(depth, MLA_KV_RANK, MLA_HEADS * HEAD_PAD)
    place = jnp.zeros((LANES, MLA_HEADS, HEAD_PAD), F32)
    j = jnp.arange(MLA_ROPE_DIM)
    place = place.at[j, :, MLA_NOPE_DIM + j].set(1.0).reshape(LANES, MLA_HEADS * HEAD_PAD)
    wkk = jnp.concatenate([wuk, jnp.broadcast_to(place, (depth,) + place.shape)], axis=1)
    wuv_t = wukv[..., MLA_NOPE_DIM:].reshape(depth, MLA_KV_RANK, MLA_WIDTH).transpose(0, 2, 1)

    return {
        "g_mix": g_mix.reshape(depth, 1, d),
        "wqv_t": jnp.concatenate([cols["q_na"], cols["v_na"]], axis=-1).transpose(0, 2, 1).astype(BF16),
        "wk": cols["k_na"].astype(BF16),
        "wc": wc.astype(BF16),
        "wg": cols["gate"].astype(BF16),
        "b_gate": b_gate.reshape(depth, 1, 2 * d),
        "g_q": g_q.reshape(depth, 1, MLA_Q_RANK),
        "g_kv": g_kv.reshape(depth, 1, MLA_KV_RANK),
        "wuq_t": wuq_t.astype(BF16),
        "wkk": wkk.astype(BF16),
        "wuv_t": wuv_t.astype(BF16),
        "w_br_na": w_br_na.astype(BF16),
        "w_br_mla": w_br_mla.astype(BF16),
        "w_out": w_out.astype(BF16),
        "g_mlp": g_mlp.reshape(depth, 1, d),
        "w1": w_ff1.astype(BF16),
        "w2": w_ff2.astype(BF16),
    }


def _rope_tables(positions):
    half = MLA_ROPE_DIM // 2
    inv_freq = 1.0 / (ROPE_THETA ** (jnp.arange(0, MLA_ROPE_DIM, 2, dtype=F32) / MLA_ROPE_DIM))
    ang_t = positions.astype(F32)[:, None, :] * inv_freq[None, :, None]
    cos_t, sin_t = jnp.cos(ang_t), jnp.sin(ang_t)
    cos, sin = cos_t.transpose(0, 2, 1), sin_t.transpose(0, 2, 1)
    pad = jnp.zeros(cos.shape[:-1] + (LANES - 2 * half,), F32)
    cos2 = jnp.concatenate([cos, cos, pad], axis=-1)
    sin2 = jnp.concatenate([-sin, sin, pad], axis=-1)
    return cos_t, sin_t, cos2, sin2


def kernel(x, c, positions, w_ada, b_ada, g_mix, w_in, b_gate, rpb, g_q, w_uq, g_kv, w_ukv, w_br_na, w_br_mla,
           w_out, g_mlp, w_ff1, w_ff2, g_final):
    batch, seq, d = x.shape
    depth = w_in.shape[0]
    grid_rows = seq // GRID_W
    assert seq % TOKEN_TILE == 0 and grid_rows % NA_Q_ROWS == 0 and grid_rows >= NA_KEY_ROWS + NA_Q_ROWS

    p = _prepare_params(g_mix, w_in, b_gate, g_q, w_uq, g_kv, w_ukv, w_br_na, w_br_mla, w_out, g_mlp,
                        w_ff1, w_ff2)
    rope = _rope_tables(positions)
    bias = _na_bias_tables(rpb, grid_rows)
    mod = _ada_mod(c, w_ada, b_ada).reshape(depth, batch, 6, d)
    g_fin = g_final.reshape(1, d)

    for layer in range(depth):
        q_na, k_na, v_na, q_m, k_m, v_m, gates = _projections(x, mod[layer], layer, p, rope)
        o_na = _na_attention(q_na, k_na, v_na, bias, layer)
        o_mla = _mla_attention(q_m, k_m, v_m)
        x = _post(x, o_na, o_mla, gates, mod[layer], layer, p, g_fin, layer == depth - 1)
    return x
```

```python
import functools
import math

import numpy as np
import jax
import jax.numpy as jnp
from jax import lax
from jax.experimental import pallas as pl
from jax.experimental.pallas import tpu as pltpu

F32 = jnp.float32
BF16 = jnp.bfloat16

GRID_W = 64
NA_HEADS = 8
NA_HEAD_DIM = 64
NA_WIN_ROWS = 8
NA_WIN_COLS = 16
MLA_HEADS = 8
MLA_Q_RANK = 256
MLA_KV_RANK = 128
MLA_NOPE_DIM = 64
MLA_ROPE_DIM = 32
MLA_V_DIM = 64
ROPE_THETA = 10000.0
NORM_EPS = 1e-6
NA_WIDTH = NA_HEADS * NA_HEAD_DIM
MLA_WIDTH = MLA_HEADS * MLA_V_DIM

LANES = 128
HEAD_PAD = 128
BF16_SUBLANES = 16
V_ROWS = NA_HEAD_DIM + BF16_SUBLANES
LOG2E = math.log2(math.e)
VMEM_LIMIT_BYTES = 56 * 1024 * 1024

TOKEN_TILE = 512
POST_TILE = 512
POST_SPLIT = 2
POST_FF_CHUNK = 2048
MLA_Q_TILE = 256
MLA_LOOKAHEAD = 3
NA_Q_ROWS = 4
NA_KEY_ROWS = 12
NA_Q_TILE = NA_Q_ROWS * GRID_W
NA_KEY_TILE = NA_KEY_ROWS * GRID_W
NA_LOOKAHEAD = 3
MASK_VALUE = -1e30


def _const_spec(shape, index_map):
    return pl.BlockSpec(shape, index_map, pipeline_mode=pl.Buffered(1))


def _rms(x, gain):
    return x * lax.rsqrt(jnp.mean(x * x, axis=-1, keepdims=True) + NORM_EPS) * gain


def _nt_dot(a, b):
    return lax.dot_general(a, b, (((1,), (1,)), ((), ())), preferred_element_type=F32)


def _ada_kernel(c_ref, w_ref, b_ref, o_ref):
    c = c_ref[...]
    c_act = (c / (1.0 + jnp.exp(-c))).astype(BF16)
    o_ref[0] = jnp.dot(c_act, w_ref[0].astype(BF16), preferred_element_type=F32) + b_ref[0]


def _ada_mod(c, w_ada, b_ada):
    depth, d, d6 = w_ada.shape
    batch = c.shape[0]
    nchunk = d6 // d
    return pl.pallas_call(
        _ada_kernel,
        grid=(depth, nchunk),
        in_specs=[
            pl.BlockSpec((batch, d), lambda l, j: (0, 0)),
            pl.BlockSpec((1, d, d), lambda l, j: (l, 0, j)),
            pl.BlockSpec((1, 1, d), lambda l, j: (l, 0, j)),
        ],
        out_specs=pl.BlockSpec((1, batch, d), lambda l, j: (l, 0, j)),
        out_shape=jax.ShapeDtypeStruct((depth, batch, d6), F32),
        name="ada_mod",
    )(c, w_ada, b_ada.reshape(depth, 1, d6))


def _modulated_norm(x, gain, shift, scale):
    return (_rms(x, gain) * (1.0 + scale) + shift).astype(BF16)


def _proj_kernel(x_ref, mod_ref, gmix_ref, wqv_ref, wk_ref, wc_ref, wg_ref, bg_ref, gq_ref, gkv_ref,
                 wuq_ref, wkk_ref, wuv_ref, cost_ref, sint_ref,
                 qna_ref, kna_ref, vna_ref, qm_ref, km_ref, vm_ref, gates_ref):
    tm = x_ref.shape[1]
    mod = mod_ref[0]
    h = _modulated_norm(x_ref[0], gmix_ref[...], mod[0:1], mod[1:2])

    qv_t = _nt_dot(wqv_ref[...], h)
    q_t = (qv_t[:NA_WIDTH] * (NA_HEAD_DIM ** -0.5 * LOG2E)).astype(BF16)
    zeros = jnp.zeros((NA_HEAD_DIM, tm), BF16)
    ones = jnp.ones((V_ROWS - NA_HEAD_DIM, NA_Q_TILE), BF16)
    v_t = qv_t[NA_WIDTH:].astype(BF16)
    for hd in range(NA_HEADS):
        q_h = q_t[hd * NA_HEAD_DIM:(hd + 1) * NA_HEAD_DIM]
        lo, hi = (q_h, zeros) if hd % 2 == 0 else (zeros, q_h)
        qna_ref[0, hd, :NA_HEAD_DIM, :] = lo
        qna_ref[0, hd, NA_HEAD_DIM:, :] = hi
        for ch in range(tm // NA_Q_TILE):
            vna_ref[0, ch, hd * V_ROWS:hd * V_ROWS + NA_HEAD_DIM, :] = (
                v_t[hd * NA_HEAD_DIM:(hd + 1) * NA_HEAD_DIM, ch * NA_Q_TILE:(ch + 1) * NA_Q_TILE])
            vna_ref[0, ch, hd * V_ROWS + NA_HEAD_DIM:(hd + 1) * V_ROWS, :] = ones
    kna_ref[0] = jnp.dot(h, wk_ref[...], preferred_element_type=F32).astype(BF16)

    logits = jnp.dot(h, wg_ref[...], preferred_element_type=F32) + bg_ref[...]
    gates_ref[0] = (1.0 / (1.0 + jnp.exp(-logits))).astype(BF16)

    lat = jnp.dot(h, wc_ref[...], preferred_element_type=F32)
    c_q = lat[:, :MLA_Q_RANK]
    c_kv = lat[:, MLA_Q_RANK:MLA_Q_RANK + MLA_KV_RANK]
    kr = lat[:, MLA_Q_RANK + MLA_KV_RANK:MLA_Q_RANK + MLA_KV_RANK + LANES]
    kr_swapped = lat[:, MLA_Q_RANK + MLA_KV_RANK + LANES:]
    cq_n = _rms(c_q, gq_ref[...]).astype(BF16)
    ckv_n = _rms(c_kv, gkv_ref[...]).astype(BF16)

    qk_dim = MLA_NOPE_DIM + MLA_ROPE_DIM
    q3 = _nt_dot(wuq_ref[...], cq_n).reshape(MLA_HEADS, qk_dim, tm)
    cos_t, sin_t = cost_ref[0], sint_ref[0]
    half = MLA_ROPE_DIM // 2
    x1 = q3[:, MLA_NOPE_DIM:MLA_NOPE_DIM + half]
    x2 = q3[:, MLA_NOPE_DIM + half:]
    q3 = jnp.concatenate([q3[:, :MLA_NOPE_DIM], x1 * cos_t - x2 * sin_t, x2 * cos_t + x1 * sin_t], axis=1)
    qm_ref[0, :, :qk_dim, :] = (q3 * (qk_dim ** -0.5 * LOG2E)).astype(BF16)
    qm_ref[0, :, qk_dim:, :] = jnp.zeros((MLA_HEADS, HEAD_PAD - qk_dim, tm), BF16)

    lane_pad = jnp.zeros((LANES - MLA_ROPE_DIM, tm), F32)
    cos2 = jnp.concatenate([cos_t, cos_t, lane_pad], axis=0).T
    sin2 = jnp.concatenate([-sin_t, sin_t, lane_pad], axis=0).T
    k_rot = (kr * cos2 + kr_swapped * sin2).astype(BF16)
    k_all = jnp.dot(jnp.concatenate([ckv_n, k_rot], axis=1), wkk_ref[...],
                    preferred_element_type=F32).astype(BF16)
    for hd in range(MLA_HEADS):
        km_ref[0, hd] = k_all[:, hd * HEAD_PAD:(hd + 1) * HEAD_PAD]
    vm_ref[0, :, 0, :MLA_V_DIM, :] = _nt_dot(wuv_ref[...], ckv_n).astype(BF16).reshape(MLA_HEADS, MLA_V_DIM, tm)
    vm_ref[0, :, 0, MLA_V_DIM:, :] = jnp.ones((MLA_HEADS, V_ROWS - MLA_V_DIM, tm), BF16)


def _projections(x, mod_l, layer, p, rope):
    batch, seq, d = x.shape
    tm = TOKEN_TILE
    nt = seq // tm
    cos_t, sin_t = rope
    lat_cols = p["wc"].shape[-1]
    tok = lambda b, t: (b, t, 0)
    per_batch = lambda b, t: (b, 0, 0)
    w3 = lambda b, t: (layer, 0, 0)
    out_shapes = (
        jax.ShapeDtypeStruct((batch, NA_HEADS, HEAD_PAD, seq), BF16),
        jax.ShapeDtypeStruct((batch, seq, NA_WIDTH), BF16),
        jax.ShapeDtypeStruct((batch, seq // NA_Q_TILE, NA_HEADS * V_ROWS, NA_Q_TILE), BF16),
        jax.ShapeDtypeStruct((batch, MLA_HEADS, HEAD_PAD, seq), BF16),
        jax.ShapeDtypeStruct((batch, MLA_HEADS, seq, HEAD_PAD), BF16),
        jax.ShapeDtypeStruct((batch, MLA_HEADS, nt, V_ROWS, tm), BF16),
        jax.ShapeDtypeStruct((batch, seq, 2 * d), BF16),
    )
    out_specs = (
        pl.BlockSpec((1, NA_HEADS, HEAD_PAD, tm), lambda b, t: (b, 0, 0, t)),
        pl.BlockSpec((1, tm, NA_WIDTH), tok),
        pl.BlockSpec((1, tm // NA_Q_TILE, NA_HEADS * V_ROWS, NA_Q_TILE), lambda b, t: (b, t, 0, 0)),
        pl.BlockSpec((1, MLA_HEADS, HEAD_PAD, tm), lambda b, t: (b, 0, 0, t)),
        pl.BlockSpec((1, MLA_HEADS, tm, HEAD_PAD), lambda b, t: (b, 0, t, 0)),
        pl.BlockSpec((1, MLA_HEADS, 1, V_ROWS, tm), lambda b, t: (b, 0, t, 0, 0)),
        pl.BlockSpec((1, tm, 2 * d), tok),
    )
    in_specs = [
        pl.BlockSpec((1, tm, d), tok),
        pl.BlockSpec((1, 6, d), per_batch),
        _const_spec((None, 1, d), w3),
        _const_spec((None, 2 * NA_WIDTH, d), w3),
        _const_spec((None, d, NA_WIDTH), w3),
        _const_spec((None, d, lat_cols), w3),
        _const_spec((None, d, 2 * d), w3),
        _const_spec((None, 1, 2 * d), w3),
        _const_spec((None, 1, MLA_Q_RANK), w3),
        _const_spec((None, 1, MLA_KV_RANK), w3),
        _const_spec((None, MLA_HEADS * (MLA_NOPE_DIM + MLA_ROPE_DIM), MLA_Q_RANK), w3),
        _const_spec((None, MLA_KV_RANK + LANES, MLA_HEADS * HEAD_PAD), w3),
        _const_spec((None, MLA_WIDTH, MLA_KV_RANK), w3),
        pl.BlockSpec((1, MLA_ROPE_DIM // 2, tm), lambda b, t: (b, 0, t)),
        pl.BlockSpec((1, MLA_ROPE_DIM // 2, tm), lambda b, t: (b, 0, t)),
    ]
    return pl.pallas_call(
        _proj_kernel,
        grid=(batch, nt),
        in_specs=in_specs,
        out_specs=out_specs,
        out_shape=out_shapes,
        compiler_params=pltpu.CompilerParams(
            dimension_semantics=("parallel", "parallel"), vmem_limit_bytes=VMEM_LIMIT_BYTES),
        name="projections",
    )(x, mod_l, p["g_mix"], p["wqv_t"], p["wk"], p["wc"], p["wg"], p["b_gate"], p["g_q"], p["g_kv"],
      p["wuq_t"], p["wkk"], p["wuv_t"], cos_t, sin_t)


def _na_kernel(q_ref, qnext_ref, k_ref, v_ref, bias_ref, o_ref, s_ref, m_ref, ot_ref, *, grid_rows):
    i = pl.program_id(1)
    nblk = pl.num_programs(1)
    nchunk = NA_KEY_TILE // NA_Q_TILE
    sub = 8
    nbuf = s_ref.shape[0]
    lookahead = nbuf - 1

    def window(step):
        key_row0 = jnp.clip(NA_Q_ROWS * step - NA_WIN_ROWS // 2, 0, grid_rows - NA_KEY_ROWS)
        pattern = jnp.where(step == 0, 0, jnp.where(step == nblk - 1, 2, 1))
        return pl.multiple_of(key_row0 * GRID_W, NA_Q_TILE), key_row0 // NA_Q_ROWS, pattern

    def scores(src_ref, win, hd):
        key0, _, pattern = win
        pair = hd // 2
        k_pair = k_ref[0, pl.ds(key0, NA_KEY_TILE), pair * LANES:(pair + 1) * LANES]
        s = jnp.dot(k_pair, src_ref[0, hd], preferred_element_type=F32)
        s = s + bias_ref[pattern, hd]
        s_ref[hd % nbuf] = s
        m_sub = jnp.max(s.reshape(NA_KEY_TILE // sub, sub, NA_Q_TILE), axis=0)
        return jnp.max(m_sub, axis=0, keepdims=True)

    def weighted_values(win, hd, m):
        _, chunk0, _ = win
        o_t = jnp.zeros((V_ROWS, NA_Q_TILE), F32)
        for ch in range(nchunk):
            p = jnp.exp2(s_ref[hd % nbuf, ch * NA_Q_TILE:(ch + 1) * NA_Q_TILE, :] - m).astype(BF16)
            v_t = v_ref[0, chunk0 + ch, hd * V_ROWS:(hd + 1) * V_ROWS, :]
            o_t = o_t + jnp.dot(v_t, p, preferred_element_type=F32)
        denom = o_t[NA_HEAD_DIM:NA_HEAD_DIM + 1]
        ot_ref[hd * NA_HEAD_DIM:(hd + 1) * NA_HEAD_DIM, :] = o_t[:NA_HEAD_DIM] * (1.0 / denom)

    cur = window(i)
    nxt = window(jnp.minimum(i + 1, nblk - 1))

    @pl.when(i == 0)
    def _first_block_of_batch():
        for hd in range(lookahead):
            m_ref[hd:hd + 1, :] = scores(q_ref, cur, hd)

    maxima = {hd: m_ref[hd:hd + 1, :] for hd in range(lookahead)}
    for hd in range(NA_HEADS):
        ahead = hd + lookahead
        if ahead < NA_HEADS:
            maxima[ahead] = scores(q_ref, cur, ahead)
        else:
            m_ref[ahead - NA_HEADS:ahead - NA_HEADS + 1, :] = scores(qnext_ref, nxt, ahead - NA_HEADS)
        weighted_values(cur, hd, maxima[hd])
    o_ref[0] = ot_ref[...].T.astype(BF16)


def _na_attention(q_t, k, v_t, bias, layer):
    batch, seq, _ = k.shape
    grid_rows = seq // GRID_W
    nblk = grid_rows // NA_Q_ROWS
    nbuf = NA_LOOKAHEAD + 1
    assert NA_HEADS % nbuf == 0
    q_block = (1, NA_HEADS, HEAD_PAD, NA_Q_TILE)
    return pl.pallas_call(
        functools.partial(_na_kernel, grid_rows=grid_rows),
        grid=(batch, nblk),
        in_specs=[
            pl.BlockSpec(q_block, lambda b, i: (b, 0, 0, i)),
            pl.BlockSpec(q_block, lambda b, i: (b, 0, 0, jnp.minimum(i + 1, nblk - 1))),
            pl.BlockSpec((1, seq, NA_WIDTH), lambda b, i: (b, 0, 0)),
            pl.BlockSpec((1, seq // NA_Q_TILE, NA_HEADS * V_ROWS, NA_Q_TILE), lambda b, i: (b, 0, 0, 0)),
            _const_spec((None, 3, NA_HEADS, NA_KEY_TILE, NA_Q_TILE), lambda b, i: (layer, 0, 0, 0, 0)),
        ],
        out_specs=pl.BlockSpec((1, NA_Q_TILE, NA_WIDTH), lambda b, i: (b, i, 0)),
        out_shape=jax.ShapeDtypeStruct((batch, seq, NA_WIDTH), BF16),
        scratch_shapes=[pltpu.VMEM((nbuf, NA_KEY_TILE, NA_Q_TILE), F32), pltpu.VMEM((NA_HEADS, NA_Q_TILE), F32),
                        pltpu.VMEM((NA_WIDTH, NA_Q_TILE), F32)],
        compiler_params=pltpu.CompilerParams(
            dimension_semantics=("parallel", "arbitrary"), vmem_limit_bytes=VMEM_LIMIT_BYTES),
        name="na_attention",
    )(q_t, q_t, k, v_t, bias)


def _na_bias_tables(rpb, grid_rows):
    depth = rpb.shape[0]
    n_col_off = 2 * NA_WIN_COLS - 1
    kc = np.arange(GRID_W)[:, None]
    c = np.arange(GRID_W)[None, :]
    cs = np.clip(c - NA_WIN_COLS // 2, 0, GRID_W - NA_WIN_COLS)
    col_ok = (kc >= cs) & (kc < cs + NA_WIN_COLS)
    col_off = kc - c + (NA_WIN_COLS - 1)
    select = ((col_off[None] == np.arange(n_col_off)[:, None, None]) & col_ok[None])
    select = select.reshape(n_col_off, GRID_W * GRID_W).astype(np.float32)
    col_tiles = jnp.dot((rpb * LOG2E).reshape(-1, n_col_off), select, precision=lax.Precision.HIGHEST)
    col_tiles = col_tiles.reshape(depth, NA_HEADS, 2 * NA_WIN_ROWS - 1, GRID_W, GRID_W)
    col_tiles = jnp.where(col_ok, col_tiles, MASK_VALUE)
    return pl.pallas_call(
        functools.partial(_bias_kernel, grid_rows=grid_rows),
        grid=(depth, NA_HEADS),
        in_specs=[pl.BlockSpec((1, 1, 2 * NA_WIN_ROWS - 1, GRID_W, GRID_W), lambda l, h: (l, h, 0, 0, 0))],
        out_specs=pl.BlockSpec((1, 3, 1, NA_KEY_TILE, NA_Q_TILE), lambda l, h: (l, 0, h, 0, 0)),
        out_shape=jax.ShapeDtypeStruct((depth, 3, NA_HEADS, NA_KEY_TILE, NA_Q_TILE), F32),
        name="na_bias",
    )(col_tiles)


def _bias_kernel(col_ref, o_ref, *, grid_rows):
    nblk = grid_rows // NA_Q_ROWS
    masked = jnp.full((GRID_W, GRID_W), MASK_VALUE, F32)
    for pattern, i in enumerate((0, 1, nblk - 1)):
        key_row0 = min(max(NA_Q_ROWS * i - NA_WIN_ROWS // 2, 0), grid_rows - NA_KEY_ROWS)
        for kr_rel in range(NA_KEY_ROWS):
            kr = key_row0 + kr_rel
            tiles = []
            for j in range(NA_Q_ROWS):
                r = NA_Q_ROWS * i + j
                rs = min(max(r - NA_WIN_ROWS // 2, 0), grid_rows - NA_WIN_ROWS)
                inside = rs <= kr < rs + NA_WIN_ROWS
                tiles.append(col_ref[0, 0, kr - r + NA_WIN_ROWS - 1] if inside else masked)
            o_ref[0, pattern, 0, kr_rel * GRID_W:(kr_rel + 1) * GRID_W, :] = jnp.concatenate(tiles, axis=1)


def _mla_kernel(q_ref, qnext_ref, k_ref, v_ref, o_ref, s_ref, m_ref, ot_ref):
    tq = q_ref.shape[-1]
    nk = v_ref.shape[2]
    tk = v_ref.shape[-1]
    sub = 8
    nbuf = s_ref.shape[0]
    lookahead = nbuf - 1

    def score_tile(src_ref, hd, j, m_sub):
        s = jnp.dot(k_ref[0, hd, j * tk:(j + 1) * tk, :], src_ref[0, hd], preferred_element_type=F32)
        s_ref[hd % nbuf, j * tk:(j + 1) * tk, :] = s
        m_j = jnp.max(s.reshape(tk // sub, sub, tq), axis=0)
        return m_j if m_sub is None else jnp.maximum(m_sub, m_j)

    def value_tile(hd, j, m, acc):
        p = jnp.exp2(s_ref[hd % nbuf, j * tk:(j + 1) * tk, :] - m).astype(BF16)
        return acc + jnp.dot(v_ref[0, hd, j], p, preferred_element_type=F32)

    @pl.when(pl.program_id(1) == 0)
    def _first_tile_of_batch():
        for hd in range(lookahead):
            m_sub = None
            for j in range(nk):
                m_sub = score_tile(q_ref, hd, j, m_sub)
            m_ref[hd:hd + 1, :] = jnp.max(m_sub, axis=0, keepdims=True)

    maxima = {hd: m_ref[hd:hd + 1, :] for hd in range(lookahead)}
    for hd in range(MLA_HEADS):
        ahead = hd + lookahead
        src_ref, ahead_hd = (q_ref, ahead) if ahead < MLA_HEADS else (qnext_ref, ahead - MLA_HEADS)
        m_sub = None
        acc = jnp.zeros((V_ROWS, tq), F32)
        for j in range(nk):
            m_sub = score_tile(src_ref, ahead_hd, j, m_sub)
            acc = value_tile(hd, j, maxima[hd], acc)
        m_ahead = jnp.max(m_sub, axis=0, keepdims=True)
        if ahead < MLA_HEADS:
            maxima[ahead] = m_ahead
        else:
            m_ref[ahead_hd:ahead_hd + 1, :] = m_ahead
        ot_ref[hd * MLA_V_DIM:(hd + 1) * MLA_V_DIM, :] = (
            acc[:MLA_V_DIM] * (1.0 / acc[MLA_V_DIM:MLA_V_DIM + 1]))
    o_ref[0] = ot_ref[...].T.astype(BF16)


def _mla_attention(q_t, k, v_t):
    batch, _, seq, _ = k.shape
    nk, tk = v_t.shape[2], v_t.shape[4]
    tq = MLA_Q_TILE
    nq = seq // tq
    nbuf = MLA_LOOKAHEAD + 1
    assert MLA_HEADS % nbuf == 0
    q_block = (1, MLA_HEADS, HEAD_PAD, tq)
    return pl.pallas_call(
        _mla_kernel,
        grid=(batch, nq),
        in_specs=[
            pl.BlockSpec(q_block, lambda b, i: (b, 0, 0, i)),
            pl.BlockSpec(q_block, lambda b, i: (b, 0, 0, jnp.minimum(i + 1, nq - 1))),
            pl.BlockSpec((1, MLA_HEADS, seq, HEAD_PAD), lambda b, i: (b, 0, 0, 0)),
            pl.BlockSpec((1, MLA_HEADS, nk, V_ROWS, tk), lambda b, i: (b, 0, 0, 0, 0)),
        ],
        out_specs=pl.BlockSpec((1, tq, MLA_WIDTH), lambda b, i: (b, i, 0)),
        out_shape=jax.ShapeDtypeStruct((batch, seq, MLA_WIDTH), BF16),
        scratch_shapes=[pltpu.VMEM((nbuf, seq, tq), F32), pltpu.VMEM((MLA_HEADS, tq), F32),
                        pltpu.VMEM((MLA_WIDTH, tq), F32)],
        compiler_params=pltpu.CompilerParams(
            dimension_semantics=("parallel", "arbitrary"), vmem_limit_bytes=VMEM_LIMIT_BYTES),
        name="mla_attention",
    )(q_t, q_t, k, v_t)


def _post_kernel(x_ref, ona_ref, omla_ref, gates_ref, mod_ref, wbn_ref, wbm_ref, wout_ref, gmlp_ref,
                 w1_ref, w2_ref, gfin_ref, o_ref, *, final):
    tm, d = x_ref.shape[1], x_ref.shape[2]
    d_ff = w1_ref.shape[-1]
    mod = mod_ref[0]
    gate_a, shift_m, scale_m, gate_m = mod[2:3], mod[3:4], mod[4:5], mod[5:6]

    def mix_tokens(rows):
        gates = gates_ref[0, rows, :]
        br_na = jnp.dot(ona_ref[0, rows, :], wbn_ref[...], preferred_element_type=F32)
        br_mla = jnp.dot(omla_ref[0, rows, :], wbm_ref[...], preferred_element_type=F32)
        merged = gates[:, :d].astype(F32) * br_na + gates[:, d:].astype(F32) * br_mla
        x = x_ref[0, rows, :] + gate_a * jnp.dot(merged.astype(BF16), wout_ref[...], preferred_element_type=F32)
        return x, _modulated_norm(x, gmlp_ref[...], shift_m, scale_m)

    def mix_channels(x, h):
        y = jnp.zeros_like(x)
        for c0 in range(0, d_ff, POST_FF_CHUNK):
            u = jnp.dot(h, w1_ref[:, c0:c0 + POST_FF_CHUNK], preferred_element_type=F32)
            u = jnp.square(jnp.maximum(u, 0.0)).astype(BF16)
            y = y + jnp.dot(u, w2_ref[c0:c0 + POST_FF_CHUNK, :], preferred_element_type=F32)
        x = x + gate_m * y
        return _rms(x, gfin_ref[...]) if final else x

    rows = [pl.ds(r0, tm // POST_SPLIT) for r0 in range(0, tm, tm // POST_SPLIT)]
    mixed = [mix_tokens(r) for r in rows]
    for r, (x, h) in zip(rows, mixed):
        o_ref[0, r, :] = mix_channels(x, h)


def _post(x, o_na, o_mla, gates, mod_l, layer, p, g_final, final):
    batch, seq, d = x.shape
    d_ff = p["w1"].shape[-1]
    tm = POST_TILE
    tok = lambda b, t: (b, t, 0)
    per_batch = lambda b, t: (b, 0, 0)
    w3 = lambda b, t: (layer, 0, 0)
    return pl.pallas_call(
        functools.partial(_post_kernel, final=final),
        grid=(batch, seq // tm),
        in_specs=[
            pl.BlockSpec((1, tm, d), tok),
            pl.BlockSpec((1, tm, NA_WIDTH), tok),
            pl.BlockSpec((1, tm, MLA_WIDTH), tok),
            pl.BlockSpec((1, tm, 2 * d), tok),
            pl.BlockSpec((1, 6, d), per_batch),
            _const_spec((None, NA_WIDTH, d), w3),
            _const_spec((None, MLA_WIDTH, d), w3),
            _const_spec((None, d, d), w3),
            _const_spec((None, 1, d), w3),
            _const_spec((None, d, d_ff), w3),
            _const_spec((None, d_ff, d), w3),
            _const_spec((1, d), lambda b, t: (0, 0)),
        ],
        out_specs=pl.BlockSpec((1, tm, d), tok),
        out_shape=jax.ShapeDtypeStruct((batch, seq, d), F32),
        compiler_params=pltpu.CompilerParams(
            dimension_semantics=("parallel", "parallel"), vmem_limit_bytes=VMEM_LIMIT_BYTES),
        name="post",
    )(x, o_na, o_mla, gates, mod_l, p["w_br_na"], p["w_br_mla"], p["w_out"], p["g_mlp"], p["w1"], p["w2"],
      g_final)


def _prepare_params(g_mix, w_in, b_gate, g_q, w_uq, g_kv, w_ukv, w_br_na, w_br_mla, w_out, g_mlp, w_ff1, w_ff2):
    depth, d, _ = w_in.shape
    o = 0
    cols = {}
    for name, size in (("q_na", NA_WIDTH), ("k_na", NA_WIDTH), ("v_na", NA_WIDTH), ("c_q", MLA_Q_RANK),
                       ("c_kv", MLA_KV_RANK), ("k_rope", MLA_ROPE_DIM), ("gate", 2 * d)):
        cols[name] = w_in[:, :, o:o + size]
        o += size
    half = MLA_ROPE_DIM // 2
    rope_pad = jnp.zeros((depth, d, LANES - MLA_ROPE_DIM), F32)
    kr = cols["k_rope"]
    kr_swapped = jnp.concatenate([kr[:, :, half:], kr[:, :, :half]], axis=-1)
    wc = jnp.concatenate([cols["c_q"], cols["c_kv"], kr, rope_pad, kr_swapped, rope_pad], axis=-1)
    wuq_t = w_uq.transpose(0, 2, 1)

    wukv = w_ukv.reshape(depth, MLA_KV_RANK, MLA_HEADS, MLA_NOPE_DIM + MLA_V_DIM)
    wuk = jnp.pad(wukv[..., :MLA_NOPE_DIM], ((0, 0), (0, 0), (0, 0), (0, HEAD_PAD - MLA_NOPE_DIM)))
    wuk = wuk.reshape(depth, MLA_KV_RANK, MLA_HEADS * HEAD_PAD)
    place = jnp.zeros((LANES, MLA_HEADS, HEAD_PAD), F32)
    j = jnp.arange(MLA_ROPE_DIM)
    place = place.at[j, :, MLA_NOPE_DIM + j].set(1.0).reshape(LANES, MLA_HEADS * HEAD_PAD)
    wkk = jnp.concatenate([wuk, jnp.broadcast_to(place, (depth,) + place.shape)], axis=1)
    wuv_t = wukv[..., MLA_NOPE_DIM:].reshape(depth, MLA_KV_RANK, MLA_WIDTH).transpose(0, 2, 1)

    return {
        "g_mix": g_mix.reshape(depth, 1, d),
        "wqv_t": jnp.concatenate([cols["q_na"], cols["v_na"]], axis=-1).transpose(0, 2, 1).astype(BF16),
        "wk": cols["k_na"].astype(BF16),
        "wc": wc.astype(BF16),
        "wg": cols["gate"].astype(BF16),
        "b_gate": b_gate.reshape(depth, 1, 2 * d),
        "g_q": g_q.reshape(depth, 1, MLA_Q_RANK),
        "g_kv": g_kv.reshape(depth, 1, MLA_KV_RANK),
        "wuq_t": wuq_t.astype(BF16),
        "wkk": wkk.astype(BF16),
        "wuv_t": wuv_t.astype(BF16),
        "w_br_na": w_br_na.astype(BF16),
        "w_br_mla": w_br_mla.astype(BF16),
        "w_out": w_out.astype(BF16),
        "g_mlp": g_mlp.reshape(depth, 1, d),
        "w1": w_ff1.astype(BF16),
        "w2": w_ff2.astype(BF16),
    }


def _rope_tables(positions):
    half = MLA_ROPE_DIM // 2
    inv_freq = 1.0 / (ROPE_THETA ** (jnp.arange(0, MLA_ROPE_DIM, 2, dtype=F32) / MLA_ROPE_DIM))
    ang_t = positions.astype(F32)[:, None, :] * inv_freq[None, :, None]
    return jnp.cos(ang_t), jnp.sin(ang_t)


def kernel(x, c, positions, w_ada, b_ada, g_mix, w_in, b_gate, rpb, g_q, w_uq, g_kv, w_ukv, w_br_na, w_br_mla,
           w_out, g_mlp, w_ff1, w_ff2, g_final):
    batch, seq, d = x.shape
    depth = w_in.shape[0]
    grid_rows = seq // GRID_W
    assert seq % TOKEN_TILE == 0 and grid_rows % NA_Q_ROWS == 0 and grid_rows >= NA_KEY_ROWS + NA_Q_ROWS

    p = _prepare_params(g_mix, w_in, b_gate, g_q, w_uq, g_kv, w_ukv, w_br_na, w_br_mla, w_out, g_mlp,
                        w_ff1, w_ff2)
    rope = _rope_tables(positions)
    bias = _na_bias_tables(rpb, grid_rows)
    mod = _ada_mod(c, w_ada, b_ada).reshape(depth, batch, 6, d)
    g_fin = g_final.reshape(1, d)

    for layer in range(depth):
        q_na, k_na, v_na, q_m, k_m, v_m, gates = _projections(x, mod[layer], layer, p, rope)
        o_na = _na_attention(q_na, k_na, v_na, bias, layer)
        o_mla = _mla_attention(q_m, k_m, v_m)
        x = _post(x, o_na, o_mla, gates, mod[layer], layer, p, g_fin, layer == depth - 1)
    return x
```

```python
import functools
import math

import numpy as np
import jax
import jax.numpy as jnp
from jax import lax
from jax.experimental import pallas as pl
from jax.experimental.pallas import tpu as pltpu

F32 = jnp.float32
BF16 = jnp.bfloat16

GRID_W = 64
NA_HEADS = 8
NA_HEAD_DIM = 64
NA_WIN_ROWS = 8
NA_WIN_COLS = 16
MLA_HEADS = 8
MLA_Q_RANK = 256
MLA_KV_RANK = 128
MLA_NOPE_DIM = 64
MLA_ROPE_DIM = 32
MLA_V_DIM = 64
ROPE_THETA = 10000.0
NORM_EPS = 1e-6
NA_WIDTH = NA_HEADS * NA_HEAD_DIM
MLA_WIDTH = MLA_HEADS * MLA_V_DIM

LANES = 128
HEAD_PAD = 128
F32_SUBLANES = 8
BF16_SUBLANES = 16
V_ROWS = NA_HEAD_DIM + BF16_SUBLANES
LOG2E = math.log2(math.e)
VMEM_LIMIT_BYTES = 56 * 1024 * 1024

TOKEN_TILE = 512
POST_TILE = 512
POST_SPLIT = 2
POST_FF_CHUNK = 2048
MLA_Q_TILE = 256
MLA_LOOKAHEAD = 3
NA_Q_ROWS = 4
NA_KEY_ROWS = 12
NA_Q_TILE = NA_Q_ROWS * GRID_W
NA_KEY_TILE = NA_KEY_ROWS * GRID_W
NA_LOOKAHEAD = 3
MASK_VALUE = -1e30


def _const_spec(shape, index_map):
    return pl.BlockSpec(shape, index_map, pipeline_mode=pl.Buffered(1))


def _rms(x, gain):
    return x * lax.rsqrt(jnp.mean(x * x, axis=-1, keepdims=True) + NORM_EPS) * gain


def _nt_dot(a, b):
    return lax.dot_general(a, b, (((1,), (1,)), ((), ())), preferred_element_type=F32)


def _tn_dot(a, b):
    return lax.dot_general(a, b, (((0,), (0,)), ((), ())), preferred_element_type=F32)


def _ada_kernel(c_ref, w_ref, b_ref, o_ref):
    c = c_ref[...]
    c_act = (c / (1.0 + jnp.exp(-c))).astype(BF16)
    o_ref[0] = jnp.dot(c_act, w_ref[0].astype(BF16), preferred_element_type=F32) + b_ref[0]


def _ada_mod(c, w_ada, b_ada):
    depth, d, d6 = w_ada.shape
    batch = c.shape[0]
    nchunk = d6 // d
    return pl.pallas_call(
        _ada_kernel,
        grid=(depth, nchunk),
        in_specs=[
            pl.BlockSpec((batch, d), lambda l, j: (0, 0)),
            pl.BlockSpec((1, d, d), lambda l, j: (l, 0, j)),
            pl.BlockSpec((1, 1, d), lambda l, j: (l, 0, j)),
        ],
        out_specs=pl.BlockSpec((1, batch, d), lambda l, j: (l, 0, j)),
        out_shape=jax.ShapeDtypeStruct((depth, batch, d6), F32),
        name="ada_mod",
    )(c, w_ada, b_ada.reshape(depth, 1, d6))


def _modulated_norm(x, gain, shift, scale):
    return (_rms(x, gain) * (1.0 + scale) + shift).astype(BF16)


def _proj_kernel(x_ref, mod_ref, gmix_ref, wqv_ref, wk_ref, wc_ref, wg_ref, bg_ref, gq_ref, gkv_ref,
                 wuq_ref, wkk_ref, wuv_ref, cost_ref, sint_ref,
                 qna_ref, kna_ref, vna_ref, qm_ref, km_ref, vm_ref, gates_ref):
    tm = x_ref.shape[1]
    mod = mod_ref[0]
    h = _modulated_norm(x_ref[0], gmix_ref[...], mod[0:1], mod[1:2])

    qv_t = _nt_dot(wqv_ref[...], h)
    q_t = (qv_t[:NA_WIDTH] * (NA_HEAD_DIM ** -0.5 * LOG2E)).astype(BF16)
    zeros = jnp.zeros((NA_HEAD_DIM, tm), BF16)
    ones = jnp.ones((V_ROWS - NA_HEAD_DIM, NA_Q_TILE), BF16)
    v_t = qv_t[NA_WIDTH:].astype(BF16)
    for hd in range(NA_HEADS):
        q_h = q_t[hd * NA_HEAD_DIM:(hd + 1) * NA_HEAD_DIM]
        lo, hi = (q_h, zeros) if hd % 2 == 0 else (zeros, q_h)
        qna_ref[0, hd, :NA_HEAD_DIM, :] = lo
        qna_ref[0, hd, NA_HEAD_DIM:, :] = hi
        for ch in range(tm // NA_Q_TILE):
            vna_ref[0, ch, hd * V_ROWS:hd * V_ROWS + NA_HEAD_DIM, :] = (
                v_t[hd * NA_HEAD_DIM:(hd + 1) * NA_HEAD_DIM, ch * NA_Q_TILE:(ch + 1) * NA_Q_TILE])
            vna_ref[0, ch, hd * V_ROWS + NA_HEAD_DIM:(hd + 1) * V_ROWS, :] = ones
    kna_ref[0] = jnp.dot(h, wk_ref[...], preferred_element_type=F32).astype(BF16)

    logits = jnp.dot(h, wg_ref[...], preferred_element_type=F32) + bg_ref[...]
    gates_ref[0] = (1.0 / (1.0 + jnp.exp(-logits))).astype(BF16)

    lat = jnp.dot(h, wc_ref[...], preferred_element_type=F32)
    c_q = lat[:, :MLA_Q_RANK]
    c_kv = lat[:, MLA_Q_RANK:MLA_Q_RANK + MLA_KV_RANK]
    kr = lat[:, MLA_Q_RANK + MLA_KV_RANK:MLA_Q_RANK + MLA_KV_RANK + LANES]
    kr_swapped = lat[:, MLA_Q_RANK + MLA_KV_RANK + LANES:]
    cq_n = _rms(c_q, gq_ref[...]).astype(BF16)
    ckv_n = _rms(c_kv, gkv_ref[...]).astype(BF16)

    qk_dim = MLA_NOPE_DIM + MLA_ROPE_DIM
    q3 = _nt_dot(wuq_ref[...], cq_n).reshape(MLA_HEADS, qk_dim, tm)
    cos_t, sin_t = cost_ref[0], sint_ref[0]
    half = MLA_ROPE_DIM // 2
    x1 = q3[:, MLA_NOPE_DIM:MLA_NOPE_DIM + half]
    x2 = q3[:, MLA_NOPE_DIM + half:]
    q3 = jnp.concatenate([q3[:, :MLA_NOPE_DIM], x1 * cos_t - x2 * sin_t, x2 * cos_t + x1 * sin_t], axis=1)
    qm_ref[0, :, :qk_dim, :] = (q3 * (qk_dim ** -0.5 * LOG2E)).astype(BF16)
    qm_ref[0, :, qk_dim:, :] = jnp.zeros((MLA_HEADS, HEAD_PAD - qk_dim, tm), BF16)

    lane_pad = jnp.zeros((LANES - MLA_ROPE_DIM, tm), F32)
    cos2 = jnp.concatenate([cos_t, cos_t, lane_pad], axis=0).T
    sin2 = jnp.concatenate([-sin_t, sin_t, lane_pad], axis=0).T
    k_rot = (kr * cos2 + kr_swapped * sin2).astype(BF16)
    k_all = jnp.dot(jnp.concatenate([ckv_n, k_rot], axis=1), wkk_ref[...],
                    preferred_element_type=F32).astype(BF16)
    for hd in range(MLA_HEADS):
        km_ref[0, hd] = k_all[:, hd * HEAD_PAD:(hd + 1) * HEAD_PAD]
    vm_ref[0, :, 0, :MLA_V_DIM, :] = _nt_dot(wuv_ref[...], ckv_n).astype(BF16).reshape(MLA_HEADS, MLA_V_DIM, tm)
    vm_ref[0, :, 0, MLA_V_DIM:, :] = jnp.ones((MLA_HEADS, V_ROWS - MLA_V_DIM, tm), BF16)


def _projections(x, mod_l, layer, p, rope):
    batch, seq, d = x.shape
    tm = TOKEN_TILE
    nt = seq // tm
    cos_t, sin_t = rope
    lat_cols = p["wc"].shape[-1]
    tok = lambda b, t: (b, t, 0)
    per_batch = lambda b, t: (b, 0, 0)
    w3 = lambda b, t: (layer, 0, 0)
    out_shapes = (
        jax.ShapeDtypeStruct((batch, NA_HEADS, HEAD_PAD, seq), BF16),
        jax.ShapeDtypeStruct((batch, seq, NA_WIDTH), BF16),
        jax.ShapeDtypeStruct((batch, seq // NA_Q_TILE, NA_HEADS * V_ROWS, NA_Q_TILE), BF16),
        jax.ShapeDtypeStruct((batch, MLA_HEADS, HEAD_PAD, seq), BF16),
        jax.ShapeDtypeStruct((batch, MLA_HEADS, seq, HEAD_PAD), BF16),
        jax.ShapeDtypeStruct((batch, MLA_HEADS, nt, V_ROWS, tm), BF16),
        jax.ShapeDtypeStruct((batch, seq, 2 * d), BF16),
    )
    out_specs = (
        pl.BlockSpec((1, NA_HEADS, HEAD_PAD, tm), lambda b, t: (b, 0, 0, t)),
        pl.BlockSpec((1, tm, NA_WIDTH), tok),
        pl.BlockSpec((1, tm // NA_Q_TILE, NA_HEADS * V_ROWS, NA_Q_TILE), lambda b, t: (b, t, 0, 0)),
        pl.BlockSpec((1, MLA_HEADS, HEAD_PAD, tm), lambda b, t: (b, 0, 0, t)),
        pl.BlockSpec((1, MLA_HEADS, tm, HEAD_PAD), lambda b, t: (b, 0, t, 0)),
        pl.BlockSpec((1, MLA_HEADS, 1, V_ROWS, tm), lambda b, t: (b, 0, t, 0, 0)),
        pl.BlockSpec((1, tm, 2 * d), tok),
    )
    in_specs = [
        pl.BlockSpec((1, tm, d), tok),
        pl.BlockSpec((1, 6, d), per_batch),
        _const_spec((None, 1, d), w3),
        _const_spec((None, 2 * NA_WIDTH, d), w3),
        _const_spec((None, d, NA_WIDTH), w3),
        _const_spec((None, d, lat_cols), w3),
        _const_spec((None, d, 2 * d), w3),
        _const_spec((None, 1, 2 * d), w3),
        _const_spec((None, 1, MLA_Q_RANK), w3),
        _const_spec((None, 1, MLA_KV_RANK), w3),
        _const_spec((None, MLA_HEADS * (MLA_NOPE_DIM + MLA_ROPE_DIM), MLA_Q_RANK), w3),
        _const_spec((None, MLA_KV_RANK + LANES, MLA_HEADS * HEAD_PAD), w3),
        _const_spec((None, MLA_WIDTH, MLA_KV_RANK), w3),
        pl.BlockSpec((1, MLA_ROPE_DIM // 2, tm), lambda b, t: (b, 0, t)),
        pl.BlockSpec((1, MLA_ROPE_DIM // 2, tm), lambda b, t: (b, 0, t)),
    ]
    return pl.pallas_call(
        _proj_kernel,
        grid=(batch, nt),
        in_specs=in_specs,
        out_specs=out_specs,
        out_shape=out_shapes,
        compiler_params=pltpu.CompilerParams(
            dimension_semantics=("parallel", "parallel"), vmem_limit_bytes=VMEM_LIMIT_BYTES),
        name="projections",
    )(x, mod_l, p["g_mix"], p["wqv_t"], p["wk"], p["wc"], p["wg"], p["b_gate"], p["g_q"], p["g_kv"],
      p["wuq_t"], p["wkk"], p["wuv_t"], cos_t, sin_t)


def _na_kernel(q_ref, qnext_ref, k_ref, v_ref, bias_ref, o_ref, s_ref, m_ref, *, grid_rows):
    i = pl.program_id(1)
    nblk = pl.num_programs(1)
    nchunk = NA_KEY_TILE // NA_Q_TILE
    sub = F32_SUBLANES
    nbuf = s_ref.shape[0]
    lookahead = nbuf - 1

    def window(step):
        key_row0 = jnp.clip(NA_Q_ROWS * step - NA_WIN_ROWS // 2, 0, grid_rows - NA_KEY_ROWS)
        pattern = jnp.where(step == 0, 0, jnp.where(step == nblk - 1, 2, 1))
        return pl.multiple_of(key_row0 * GRID_W, NA_Q_TILE), key_row0 // NA_Q_ROWS, pattern

    def scores(src_ref, win, hd):
        key0, _, pattern = win
        pair = hd // 2
        k_pair = k_ref[0, pl.ds(key0, NA_KEY_TILE), pair * LANES:(pair + 1) * LANES]
        s = jnp.dot(k_pair, src_ref[0, hd], preferred_element_type=F32)
        s = s + bias_ref[pattern, hd]
        s_ref[hd % nbuf] = s
        m_sub = jnp.max(s.reshape(NA_KEY_TILE // sub, sub, NA_Q_TILE), axis=0)
        return jnp.max(m_sub, axis=0, keepdims=True)

    def weighted_values(win, hd, m):
        _, chunk0, _ = win
        o_t = jnp.zeros((V_ROWS, NA_Q_TILE), F32)
        for ch in range(nchunk):
            p = jnp.exp2(s_ref[hd % nbuf, ch * NA_Q_TILE:(ch + 1) * NA_Q_TILE, :] - m).astype(BF16)
            v_t = v_ref[0, chunk0 + ch, hd * V_ROWS:(hd + 1) * V_ROWS, :]
            o_t = o_t + jnp.dot(v_t, p, preferred_element_type=F32)
        denom = o_t[NA_HEAD_DIM:NA_HEAD_DIM + 1]
        o_ref[0, hd * NA_HEAD_DIM:(hd + 1) * NA_HEAD_DIM, :] = (o_t[:NA_HEAD_DIM] * (1.0 / denom)).astype(BF16)

    cur = window(i)
    nxt = window(jnp.minimum(i + 1, nblk - 1))

    @pl.when(i == 0)
    def _first_block_of_batch():
        for hd in range(lookahead):
            m_ref[hd:hd + 1, :] = scores(q_ref, cur, hd)

    maxima = {hd: m_ref[hd:hd + 1, :] for hd in range(lookahead)}
    for hd in range(NA_HEADS):
        ahead = hd + lookahead
        if ahead < NA_HEADS:
            maxima[ahead] = scores(q_ref, cur, ahead)
        else:
            m_ref[ahead - NA_HEADS:ahead - NA_HEADS + 1, :] = scores(qnext_ref, nxt, ahead - NA_HEADS)
        weighted_values(cur, hd, maxima[hd])


def _na_attention(q_t, k, v_t, bias, layer):
    batch, seq, _ = k.shape
    grid_rows = seq // GRID_W
    nblk = grid_rows // NA_Q_ROWS
    nbuf = NA_LOOKAHEAD + 1
    assert NA_HEADS % nbuf == 0
    q_block = (1, NA_HEADS, HEAD_PAD, NA_Q_TILE)
    return pl.pallas_call(
        functools.partial(_na_kernel, grid_rows=grid_rows),
        grid=(batch, nblk),
        in_specs=[
            pl.BlockSpec(q_block, lambda b, i: (b, 0, 0, i)),
            pl.BlockSpec(q_block, lambda b, i: (b, 0, 0, jnp.minimum(i + 1, nblk - 1))),
            pl.BlockSpec((1, seq, NA_WIDTH), lambda b, i: (b, 0, 0)),
            pl.BlockSpec((1, seq // NA_Q_TILE, NA_HEADS * V_ROWS, NA_Q_TILE), lambda b, i: (b, 0, 0, 0)),
            _const_spec((None, 3, NA_HEADS, NA_KEY_TILE, NA_Q_TILE), lambda b, i: (layer, 0, 0, 0, 0)),
        ],
        out_specs=pl.BlockSpec((1, NA_WIDTH, NA_Q_TILE), lambda b, i: (b, 0, i)),
        out_shape=jax.ShapeDtypeStruct((batch, NA_WIDTH, seq), BF16),
        scratch_shapes=[pltpu.VMEM((nbuf, NA_KEY_TILE, NA_Q_TILE), F32), pltpu.VMEM((NA_HEADS, NA_Q_TILE), F32)],
        compiler_params=pltpu.CompilerParams(
            dimension_semantics=("parallel", "arbitrary"), vmem_limit_bytes=VMEM_LIMIT_BYTES),
        name="na_attention",
    )(q_t, q_t, k, v_t, bias)


def _na_bias_tables(rpb, grid_rows):
    depth = rpb.shape[0]
    n_col_off = 2 * NA_WIN_COLS - 1
    kc = np.arange(GRID_W)[:, None]
    c = np.arange(GRID_W)[None, :]
    cs = np.clip(c - NA_WIN_COLS // 2, 0, GRID_W - NA_WIN_COLS)
    col_ok = (kc >= cs) & (kc < cs + NA_WIN_COLS)
    col_off = kc - c + (NA_WIN_COLS - 1)
    select = ((col_off[None] == np.arange(n_col_off)[:, None, None]) & col_ok[None])
    select = select.reshape(n_col_off, GRID_W * GRID_W).astype(np.float32)
    col_tiles = jnp.dot((rpb * LOG2E).reshape(-1, n_col_off), select, precision=lax.Precision.HIGHEST)
    col_tiles = col_tiles.reshape(depth, NA_HEADS, 2 * NA_WIN_ROWS - 1, GRID_W, GRID_W)
    col_tiles = jnp.where(col_ok, col_tiles, MASK_VALUE)
    return pl.pallas_call(
        functools.partial(_bias_kernel, grid_rows=grid_rows),
        grid=(depth, NA_HEADS),
        in_specs=[pl.BlockSpec((1, 1, 2 * NA_WIN_ROWS - 1, GRID_W, GRID_W), lambda l, h: (l, h, 0, 0, 0))],
        out_specs=pl.BlockSpec((1, 3, 1, NA_KEY_TILE, NA_Q_TILE), lambda l, h: (l, 0, h, 0, 0)),
        out_shape=jax.ShapeDtypeStruct((depth, 3, NA_HEADS, NA_KEY_TILE, NA_Q_TILE), F32),
        name="na_bias",
    )(col_tiles)


def _bias_kernel(col_ref, o_ref, *, grid_rows):
    nblk = grid_rows // NA_Q_ROWS
    masked = jnp.full((GRID_W, GRID_W), MASK_VALUE, F32)
    for pattern, i in enumerate((0, 1, nblk - 1)):
        key_row0 = min(max(NA_Q_ROWS * i - NA_WIN_ROWS // 2, 0), grid_rows - NA_KEY_ROWS)
        for kr_rel in range(NA_KEY_ROWS):
            kr = key_row0 + kr_rel
            tiles = []
            for j in range(NA_Q_ROWS):
                r = NA_Q_ROWS * i + j
                rs = min(max(r - NA_WIN_ROWS // 2, 0), grid_rows - NA_WIN_ROWS)
                inside = rs <= kr < rs + NA_WIN_ROWS
                tiles.append(col_ref[0, 0, kr - r + NA_WIN_ROWS - 1] if inside else masked)
            o_ref[0, pattern, 0, kr_rel * GRID_W:(kr_rel + 1) * GRID_W, :] = jnp.concatenate(tiles, axis=1)


def _mla_kernel(q_ref, qnext_ref, k_ref, v_ref, o_ref, s_ref, m_ref):
    tq = q_ref.shape[-1]
    nk = v_ref.shape[2]
    tk = v_ref.shape[-1]
    sub = F32_SUBLANES
    nbuf = s_ref.shape[0]
    lookahead = nbuf - 1

    def score_tile(src_ref, hd, j, m_sub):
        s = jnp.dot(k_ref[0, hd, j * tk:(j + 1) * tk, :], src_ref[0, hd], preferred_element_type=F32)
        s_ref[hd % nbuf, j * tk:(j + 1) * tk, :] = s
        m_j = jnp.max(s.reshape(tk // sub, sub, tq), axis=0)
        return m_j if m_sub is None else jnp.maximum(m_sub, m_j)

    def value_tile(hd, j, m, acc):
        p = jnp.exp2(s_ref[hd % nbuf, j * tk:(j + 1) * tk, :] - m).astype(BF16)
        return acc + jnp.dot(v_ref[0, hd, j], p, preferred_element_type=F32)

    @pl.when(pl.program_id(1) == 0)
    def _first_tile_of_batch():
        for hd in range(lookahead):
            m_sub = None
            for j in range(nk):
                m_sub = score_tile(q_ref, hd, j, m_sub)
            m_ref[hd:hd + 1, :] = jnp.max(m_sub, axis=0, keepdims=True)

    maxima = {hd: m_ref[hd:hd + 1, :] for hd in range(lookahead)}
    for hd in range(MLA_HEADS):
        ahead = hd + lookahead
        src_ref, ahead_hd = (q_ref, ahead) if ahead < MLA_HEADS else (qnext_ref, ahead - MLA_HEADS)
        m_sub = None
        acc = jnp.zeros((V_ROWS, tq), F32)
        for j in range(nk):
            m_sub = score_tile(src_ref, ahead_hd, j, m_sub)
            acc = value_tile(hd, j, maxima[hd], acc)
        m_ahead = jnp.max(m_sub, axis=0, keepdims=True)
        if ahead < MLA_HEADS:
            maxima[ahead] = m_ahead
        else:
            m_ref[ahead_hd:ahead_hd + 1, :] = m_ahead
        o_ref[0, hd * MLA_V_DIM:(hd + 1) * MLA_V_DIM, :] = (
            acc[:MLA_V_DIM] * (1.0 / acc[MLA_V_DIM:MLA_V_DIM + 1])).astype(BF16)


def _mla_attention(q_t, k, v_t):
    batch, _, seq, _ = k.shape
    nk, tk = v_t.shape[2], v_t.shape[4]
    tq = MLA_Q_TILE
    nq = seq // tq
    nbuf = MLA_LOOKAHEAD + 1
    assert MLA_HEADS % nbuf == 0
    q_block = (1, MLA_HEADS, HEAD_PAD, tq)
    return pl.pallas_call(
        _mla_kernel,
        grid=(batch, nq),
        in_specs=[
            pl.BlockSpec(q_block, lambda b, i: (b, 0, 0, i)),
            pl.BlockSpec(q_block, lambda b, i: (b, 0, 0, jnp.minimum(i + 1, nq - 1))),
            pl.BlockSpec((1, MLA_HEADS, seq, HEAD_PAD), lambda b, i: (b, 0, 0, 0)),
            pl.BlockSpec((1, MLA_HEADS, nk, V_ROWS, tk), lambda b, i: (b, 0, 0, 0, 0)),
        ],
        out_specs=pl.BlockSpec((1, MLA_WIDTH, tq), lambda b, i: (b, 0, i)),
        out_shape=jax.ShapeDtypeStruct((batch, MLA_WIDTH, seq), BF16),
        scratch_shapes=[pltpu.VMEM((nbuf, seq, tq), F32), pltpu.VMEM((MLA_HEADS, tq), F32)],
        compiler_params=pltpu.CompilerParams(
            dimension_semantics=("parallel", "arbitrary"), vmem_limit_bytes=VMEM_LIMIT_BYTES),
        name="mla_attention",
    )(q_t, q_t, k, v_t)


def _post_kernel(x_ref, ona_ref, omla_ref, gates_ref, mod_ref, wbn_ref, wbm_ref, wout_ref, gmlp_ref,
                 w1_ref, w2_ref, gfin_ref, o_ref, *, final):
    tm, d = x_ref.shape[1], x_ref.shape[2]
    d_ff = w1_ref.shape[-1]
    mod = mod_ref[0]
    gate_a, shift_m, scale_m, gate_m = mod[2:3], mod[3:4], mod[4:5], mod[5:6]

    def mix_tokens(rows):
        gates = gates_ref[0, rows, :]
        br_na = _tn_dot(ona_ref[0, :, rows], wbn_ref[...])
        br_mla = _tn_dot(omla_ref[0, :, rows], wbm_ref[...])
        merged = gates[:, :d].astype(F32) * br_na + gates[:, d:].astype(F32) * br_mla
        x = x_ref[0, rows, :] + gate_a * jnp.dot(merged.astype(BF16), wout_ref[...], preferred_element_type=F32)
        return x, _modulated_norm(x, gmlp_ref[...], shift_m, scale_m)

    def mix_channels(x, h):
        y = jnp.zeros_like(x)
        for c0 in range(0, d_ff, POST_FF_CHUNK):
            u = jnp.dot(h, w1_ref[:, c0:c0 + POST_FF_CHUNK], preferred_element_type=F32)
            u = jnp.square(jnp.maximum(u, 0.0)).astype(BF16)
            y = y + jnp.dot(u, w2_ref[c0:c0 + POST_FF_CHUNK, :], preferred_element_type=F32)
        x = x + gate_m * y
        return _rms(x, gfin_ref[...]) if final else x

    rows = [pl.ds(r0, tm // POST_SPLIT) for r0 in range(0, tm, tm // POST_SPLIT)]
    mixed = [mix_tokens(r) for r in rows]
    for r, (x, h) in zip(rows, mixed):
        o_ref[0, r, :] = mix_channels(x, h)


def _post(x, o_na, o_mla, gates, mod_l, layer, p, g_final, final):
    batch, seq, d = x.shape
    d_ff = p["w1"].shape[-1]
    tm = POST_TILE
    tok = lambda b, t: (b, t, 0)
    per_batch = lambda b, t: (b, 0, 0)
    w3 = lambda b, t: (layer, 0, 0)
    return pl.pallas_call(
        functools.partial(_post_kernel, final=final),
        grid=(batch, seq // tm),
        in_specs=[
            pl.BlockSpec((1, tm, d), tok),
            pl.BlockSpec((1, NA_WIDTH, tm), lambda b, t: (b, 0, t)),
            pl.BlockSpec((1, MLA_WIDTH, tm), lambda b, t: (b, 0, t)),
            pl.BlockSpec((1, tm, 2 * d), tok),
            pl.BlockSpec((1, 6, d), per_batch),
            _const_spec((None, NA_WIDTH, d), w3),
            _const_spec((None, MLA_WIDTH, d), w3),
            _const_spec((None, d, d), w3),
            _const_spec((None, 1, d), w3),
            _const_spec((None, d, d_ff), w3),
            _const_spec((None, d_ff, d), w3),
            _const_spec((1, d), lambda b, t: (0, 0)),
        ],
        out_specs=pl.BlockSpec((1, tm, d), tok),
        out_shape=jax.ShapeDtypeStruct((batch, seq, d), F32),
        compiler_params=pltpu.CompilerParams(
            dimension_semantics=("parallel", "parallel"), vmem_limit_bytes=VMEM_LIMIT_BYTES),
        name="post",
    )(x, o_na, o_mla, gates, mod_l, p["w_br_na"], p["w_br_mla"], p["w_out"], p["g_mlp"], p["w1"], p["w2"],
      g_final)


def _prepare_params(g_mix, w_in, b_gate, g_q, w_uq, g_kv, w_ukv, w_br_na, w_br_mla, w_out, g_mlp, w_ff1, w_ff2):
    depth, d, _ = w_in.shape
    o = 0
    cols = {}
    for name, size in (("q_na", NA_WIDTH), ("k_na", NA_WIDTH), ("v_na", NA_WIDTH), ("c_q", MLA_Q_RANK),
                       ("c_kv", MLA_KV_RANK), ("k_rope", MLA_ROPE_DIM), ("gate", 2 * d)):
        cols[name] = w_in[:, :, o:o + size]
        o += size
    half = MLA_ROPE_DIM // 2
    rope_pad = jnp.zeros((depth, d, LANES - MLA_ROPE_DIM), F32)
    kr = cols["k_rope"]
    kr_swapped = jnp.concatenate([kr[:, :, half:], kr[:, :, :half]], axis=-1)
    wc = jnp.concatenate([cols["c_q"], cols["c_kv"], kr, rope_pad, kr_swapped, rope_pad], axis=-1)
    wuq_t = w_uq.transpose(0, 2, 1)

    wukv = w_ukv.reshape(depth, MLA_KV_RANK, MLA_HEADS, MLA_NOPE_DIM + MLA_V_DIM)
    wuk = jnp.pad(wukv[..., :MLA_NOPE_DIM], ((0, 0), (0, 0), (0, 0), (0, HEAD_PAD - MLA_NOPE_DIM)))
    wuk = wuk.reshape(depth, MLA_KV_RANK, MLA_HEADS * HEAD_PAD)
    place = jnp.zeros((LANES, MLA_HEADS, HEAD_PAD), F32)
    j = jnp.arange(MLA_ROPE_DIM)
    place = place.at[j, :, MLA_NOPE_DIM + j].set(1.0).reshape(LANES, MLA_HEADS * HEAD_PAD)
    wkk = jnp.concatenate([wuk, jnp.broadcast_to(place, (depth,) + place.shape)], axis=1)
    wuv_t = wukv[..., MLA_NOPE_DIM:].reshape(depth, MLA_KV_RANK, MLA_WIDTH).transpose(0, 2, 1)

    return {
        "g_mix": g_mix.reshape(depth, 1, d),
        "wqv_t": jnp.concatenate([cols["q_na"], cols["v_na"]], axis=-1).transpose(0, 2, 1).astype(BF16),
        "wk": cols["k_na"].astype(BF16),
        "wc": wc.astype(BF16),
        "wg": cols["gate"].astype(BF16),
        "b_gate": b_gate.reshape(depth, 1, 2 * d),
        "g_q": g_q.reshape(depth, 1, MLA_Q_RANK),
        "g_kv": g_kv.reshape(depth, 1, MLA_KV_RANK),
        "wuq_t": wuq_t.astype(BF16),
        "wkk": wkk.astype(BF16),
        "wuv_t": wuv_t.astype(BF16),
        "w_br_na": w_br_na.astype(BF16),
        "w_br_mla": w_br_mla.astype(BF16),
        "w_out": w_out.astype(BF16),
        "g_mlp": g_mlp.reshape(depth, 1, d),
        "w1": w_ff1.astype(BF16),
        "w2": w_ff2.astype(BF16),
    }


def _rope_tables(positions):
    inv_freq = 1.0 / (ROPE_THETA ** (jnp.arange(0, MLA_ROPE_DIM, 2, dtype=F32) / MLA_ROPE_DIM))
    ang_t = positions.astype(F32)[:, None, :] * inv_freq[None, :, None]
    return jnp.cos(ang_t), jnp.sin(ang_t)


def kernel(x, c, positions, w_ada, b_ada, g_mix, w_in, b_gate, rpb, g_q, w_uq, g_kv, w_ukv, w_br_na, w_br_mla,
           w_out, g_mlp, w_ff1, w_ff2, g_final):
    batch, seq, d = x.shape
    depth = w_in.shape[0]
    grid_rows = seq // GRID_W
    assert seq % TOKEN_TILE == 0 and grid_rows % NA_Q_ROWS == 0 and grid_rows >= NA_KEY_ROWS + NA_Q_ROWS

    p = _prepare_params(g_mix, w_in, b_gate, g_q, w_uq, g_kv, w_ukv, w_br_na, w_br_mla, w_out, g_mlp,
                        w_ff1, w_ff2)
    rope = _rope_tables(positions)
    bias = _na_bias_tables(rpb, grid_rows)
    mod = _ada_mod(c, w_ada, b_ada).reshape(depth, batch, 6, d)
    g_fin = g_final.reshape(1, d)

    for layer in range(depth):
        q_na, k_na, v_na, q_m, k_m, v_m, gates = _projections(x, mod[layer], layer, p, rope)
        o_na = _na_attention(q_na, k_na, v_na, bias, layer)
        o_mla = _mla_attention(q_m, k_m, v_m)
        x = _post(x, o_na, o_mla, gates, mod[layer], layer, p, g_fin, layer == depth - 1)
    return x
```

```python
import functools
import math

import numpy as np
import jax
import jax.numpy as jnp
from jax import lax
from jax.experimental import pallas as pl
from jax.experimental.pallas import tpu as pltpu

F32 = jnp.float32
BF16 = jnp.bfloat16

GRID_W = 64
NA_HEADS = 8
NA_HEAD_DIM = 64
NA_WIN_ROWS = 8
NA_WIN_COLS = 16
MLA_HEADS = 8
MLA_Q_RANK = 256
MLA_KV_RANK = 128
MLA_NOPE_DIM = 64
MLA_ROPE_DIM = 32
MLA_V_DIM = 64
ROPE_THETA = 10000.0
NORM_EPS = 1e-6
NA_WIDTH = NA_HEADS * NA_HEAD_DIM
MLA_WIDTH = MLA_HEADS * MLA_V_DIM

LANES = 128
HEAD_PAD = 128
F32_SUBLANES = 8
BF16_SUBLANES = 16
V_ROWS = NA_HEAD_DIM + BF16_SUBLANES
LOG2E = math.log2(math.e)
VMEM_LIMIT_BYTES = 56 * 1024 * 1024

TOKEN_TILE = 512
POST_TILE = 512
POST_SPLIT = 2
POST_FF_CHUNK = 2048
MLA_Q_TILE = 256
MLA_TILES_PER_STEP = 2
MLA_LOOKAHEAD = 3
NA_Q_ROWS = 4
NA_KEY_ROWS = 12
NA_Q_TILE = NA_Q_ROWS * GRID_W
NA_KEY_TILE = NA_KEY_ROWS * GRID_W
NA_BLOCKS_PER_STEP = 4
NA_LOOKAHEAD = 3
MASK_VALUE = -1e30


def _const_spec(shape, index_map):
    return pl.BlockSpec(shape, index_map, pipeline_mode=pl.Buffered(1))


def _rms(x, gain):
    return x * lax.rsqrt(jnp.mean(x * x, axis=-1, keepdims=True) + NORM_EPS) * gain


def _nt_dot(a, b):
    return lax.dot_general(a, b, (((1,), (1,)), ((), ())), preferred_element_type=F32)


def _tn_dot(a, b):
    return lax.dot_general(a, b, (((0,), (0,)), ((), ())), preferred_element_type=F32)


def _ada_kernel(c_ref, w_ref, b_ref, o_ref):
    c = c_ref[...]
    c_act = (c / (1.0 + jnp.exp(-c))).astype(BF16)
    o_ref[0] = jnp.dot(c_act, w_ref[0].astype(BF16), preferred_element_type=F32) + b_ref[0]


def _ada_mod(c, w_ada, b_ada):
    depth, d, d6 = w_ada.shape
    batch = c.shape[0]
    nchunk = d6 // d
    return pl.pallas_call(
        _ada_kernel,
        grid=(depth, nchunk),
        in_specs=[
            pl.BlockSpec((batch, d), lambda l, j: (0, 0)),
            pl.BlockSpec((1, d, d), lambda l, j: (l, 0, j)),
            pl.BlockSpec((1, 1, d), lambda l, j: (l, 0, j)),
        ],
        out_specs=pl.BlockSpec((1, batch, d), lambda l, j: (l, 0, j)),
        out_shape=jax.ShapeDtypeStruct((depth, batch, d6), F32),
        name="ada_mod",
    )(c, w_ada, b_ada.reshape(depth, 1, d6))


def _modulated_norm(x, gain, shift, scale):
    return (_rms(x, gain) * (1.0 + scale) + shift).astype(BF16)


def _proj_kernel(x_ref, mod_ref, gmix_ref, wqv_ref, wk_ref, wc_ref, wg_ref, bg_ref, gq_ref, gkv_ref,
                 wuq_ref, wkk_ref, wuv_ref, cost_ref, sint_ref,
                 qna_ref, kna_ref, vna_ref, qm_ref, km_ref, vm_ref, gates_ref):
    tm = x_ref.shape[1]
    mod = mod_ref[0]
    h = _modulated_norm(x_ref[0], gmix_ref[...], mod[0:1], mod[1:2])

    qv_t = _nt_dot(wqv_ref[...], h)
    q_t = (qv_t[:NA_WIDTH] * (NA_HEAD_DIM ** -0.5 * LOG2E)).astype(BF16)
    zeros = jnp.zeros((NA_HEAD_DIM, tm), BF16)
    ones = jnp.ones((V_ROWS - NA_HEAD_DIM, NA_Q_TILE), BF16)
    v_t = qv_t[NA_WIDTH:].astype(BF16)
    for hd in range(NA_HEADS):
        q_h = q_t[hd * NA_HEAD_DIM:(hd + 1) * NA_HEAD_DIM]
        lo, hi = (q_h, zeros) if hd % 2 == 0 else (zeros, q_h)
        qna_ref[0, hd, :NA_HEAD_DIM, :] = lo
        qna_ref[0, hd, NA_HEAD_DIM:, :] = hi
        for ch in range(tm // NA_Q_TILE):
            vna_ref[0, ch, hd * V_ROWS:hd * V_ROWS + NA_HEAD_DIM, :] = (
                v_t[hd * NA_HEAD_DIM:(hd + 1) * NA_HEAD_DIM, ch * NA_Q_TILE:(ch + 1) * NA_Q_TILE])
            vna_ref[0, ch, hd * V_ROWS + NA_HEAD_DIM:(hd + 1) * V_ROWS, :] = ones
    kna_ref[0] = jnp.dot(h, wk_ref[...], preferred_element_type=F32).astype(BF16)

    logits = jnp.dot(h, wg_ref[...], preferred_element_type=F32) + bg_ref[...]
    gates_ref[0] = (1.0 / (1.0 + jnp.exp(-logits))).astype(BF16)

    lat = jnp.dot(h, wc_ref[...], preferred_element_type=F32)
    c_q = lat[:, :MLA_Q_RANK]
    c_kv = lat[:, MLA_Q_RANK:MLA_Q_RANK + MLA_KV_RANK]
    kr = lat[:, MLA_Q_RANK + MLA_KV_RANK:MLA_Q_RANK + MLA_KV_RANK + LANES]
    kr_swapped = lat[:, MLA_Q_RANK + MLA_KV_RANK + LANES:]
    cq_n = _rms(c_q, gq_ref[...]).astype(BF16)
    ckv_n = _rms(c_kv, gkv_ref[...]).astype(BF16)

    qk_dim = MLA_NOPE_DIM + MLA_ROPE_DIM
    q3 = _nt_dot(wuq_ref[...], cq_n).reshape(MLA_HEADS, qk_dim, tm)
    cos_t, sin_t = cost_ref[0], sint_ref[0]
    half = MLA_ROPE_DIM // 2
    x1 = q3[:, MLA_NOPE_DIM:MLA_NOPE_DIM + half]
    x2 = q3[:, MLA_NOPE_DIM + half:]
    q3 = jnp.concatenate([q3[:, :MLA_NOPE_DIM], x1 * cos_t - x2 * sin_t, x2 * cos_t + x1 * sin_t], axis=1)
    qm_ref[0, :, :qk_dim, :] = (q3 * (qk_dim ** -0.5 * LOG2E)).astype(BF16)
    qm_ref[0, :, qk_dim:, :] = jnp.zeros((MLA_HEADS, HEAD_PAD - qk_dim, tm), BF16)

    lane_pad = jnp.zeros((LANES - MLA_ROPE_DIM, tm), F32)
    cos2 = jnp.concatenate([cos_t, cos_t, lane_pad], axis=0).T
    sin2 = jnp.concatenate([-sin_t, sin_t, lane_pad], axis=0).T
    k_rot = (kr * cos2 + kr_swapped * sin2).astype(BF16)
    k_all = jnp.dot(jnp.concatenate([ckv_n, k_rot], axis=1), wkk_ref[...],
                    preferred_element_type=F32).astype(BF16)
    for hd in range(MLA_HEADS):
        km_ref[0, hd] = k_all[:, hd * HEAD_PAD:(hd + 1) * HEAD_PAD]
    vm_ref[0, :, 0, :MLA_V_DIM, :] = _nt_dot(wuv_ref[...], ckv_n).astype(BF16).reshape(MLA_HEADS, MLA_V_DIM, tm)
    vm_ref[0, :, 0, MLA_V_DIM:, :] = jnp.ones((MLA_HEADS, V_ROWS - MLA_V_DIM, tm), BF16)


def _projections(x, mod_l, layer, p, rope):
    batch, seq, d = x.shape
    tm = TOKEN_TILE
    nt = seq // tm
    cos_t, sin_t = rope
    lat_cols = p["wc"].shape[-1]
    tok = lambda b, t: (b, t, 0)
    per_batch = lambda b, t: (b, 0, 0)
    w3 = lambda b, t: (layer, 0, 0)
    out_shapes = (
        jax.ShapeDtypeStruct((batch, NA_HEADS, HEAD_PAD, seq), BF16),
        jax.ShapeDtypeStruct((batch, seq, NA_WIDTH), BF16),
        jax.ShapeDtypeStruct((batch, seq // NA_Q_TILE, NA_HEADS * V_ROWS, NA_Q_TILE), BF16),
        jax.ShapeDtypeStruct((batch, MLA_HEADS, HEAD_PAD, seq), BF16),
        jax.ShapeDtypeStruct((batch, MLA_HEADS, seq, HEAD_PAD), BF16),
        jax.ShapeDtypeStruct((batch, MLA_HEADS, nt, V_ROWS, tm), BF16),
        jax.ShapeDtypeStruct((batch, seq, 2 * d), BF16),
    )
    out_specs = (
        pl.BlockSpec((1, NA_HEADS, HEAD_PAD, tm), lambda b, t: (b, 0, 0, t)),
        pl.BlockSpec((1, tm, NA_WIDTH), tok),
        pl.BlockSpec((1, tm // NA_Q_TILE, NA_HEADS * V_ROWS, NA_Q_TILE), lambda b, t: (b, t, 0, 0)),
        pl.BlockSpec((1, MLA_HEADS, HEAD_PAD, tm), lambda b, t: (b, 0, 0, t)),
        pl.BlockSpec((1, MLA_HEADS, tm, HEAD_PAD), lambda b, t: (b, 0, t, 0)),
        pl.BlockSpec((1, MLA_HEADS, 1, V_ROWS, tm), lambda b, t: (b, 0, t, 0, 0)),
        pl.BlockSpec((1, tm, 2 * d), tok),
    )
    in_specs = [
        pl.BlockSpec((1, tm, d), tok),
        pl.BlockSpec((1, 6, d), per_batch),
        _const_spec((None, 1, d), w3),
        _const_spec((None, 2 * NA_WIDTH, d), w3),
        _const_spec((None, d, NA_WIDTH), w3),
        _const_spec((None, d, lat_cols), w3),
        _const_spec((None, d, 2 * d), w3),
        _const_spec((None, 1, 2 * d), w3),
        _const_spec((None, 1, MLA_Q_RANK), w3),
        _const_spec((None, 1, MLA_KV_RANK), w3),
        _const_spec((None, MLA_HEADS * (MLA_NOPE_DIM + MLA_ROPE_DIM), MLA_Q_RANK), w3),
        _const_spec((None, MLA_KV_RANK + LANES, MLA_HEADS * HEAD_PAD), w3),
        _const_spec((None, MLA_WIDTH, MLA_KV_RANK), w3),
        pl.BlockSpec((1, MLA_ROPE_DIM // 2, tm), lambda b, t: (b, 0, t)),
        pl.BlockSpec((1, MLA_ROPE_DIM // 2, tm), lambda b, t: (b, 0, t)),
    ]
    return pl.pallas_call(
        _proj_kernel,
        grid=(batch, nt),
        in_specs=in_specs,
        out_specs=out_specs,
        out_shape=out_shapes,
        compiler_params=pltpu.CompilerParams(
            dimension_semantics=("parallel", "parallel"), vmem_limit_bytes=VMEM_LIMIT_BYTES),
        name="projections",
    )(x, mod_l, p["g_mix"], p["wqv_t"], p["wk"], p["wc"], p["wg"], p["b_gate"], p["g_q"], p["g_kv"],
      p["wuq_t"], p["wkk"], p["wuv_t"], cos_t, sin_t)


def _na_kernel(q_ref, qnext_ref, k_ref, v_ref, bias_ref, o_ref, s_ref, m_ref, *, grid_rows):
    i = pl.program_id(1)
    nblk = pl.num_programs(1) * NA_BLOCKS_PER_STEP
    nchunk = NA_KEY_TILE // NA_Q_TILE
    sub = F32_SUBLANES
    nbuf = s_ref.shape[0]
    lookahead = nbuf - 1
    items = [(blk, hd) for blk in range(NA_BLOCKS_PER_STEP) for hd in range(NA_HEADS)]

    def window(block):
        key_row0 = jnp.clip(NA_Q_ROWS * block - NA_WIN_ROWS // 2, 0, grid_rows - NA_KEY_ROWS)
        pattern = jnp.where(block == 0, 0, jnp.where(block == nblk - 1, 2, 1))
        return pl.multiple_of(key_row0 * GRID_W, NA_Q_TILE), key_row0 // NA_Q_ROWS, pattern

    def scores(src_ref, win, item, slot):
        blk, hd = item
        key0, _, pattern = win
        pair = hd // 2
        k_pair = k_ref[0, pl.ds(key0, NA_KEY_TILE), pair * LANES:(pair + 1) * LANES]
        q_t = src_ref[0, hd, :, blk * NA_Q_TILE:(blk + 1) * NA_Q_TILE]
        s = jnp.dot(k_pair, q_t, preferred_element_type=F32)
        s = s + bias_ref[pattern, hd]
        s_ref[slot] = s
        m_sub = jnp.max(s.reshape(NA_KEY_TILE // sub, sub, NA_Q_TILE), axis=0)
        return jnp.max(m_sub, axis=0, keepdims=True)

    def weighted_values(win, item, slot, m):
        blk, hd = item
        _, chunk0, _ = win
        o_t = jnp.zeros((V_ROWS, NA_Q_TILE), F32)
        for ch in range(nchunk):
            p = jnp.exp2(s_ref[slot, ch * NA_Q_TILE:(ch + 1) * NA_Q_TILE, :] - m).astype(BF16)
            v_t = v_ref[0, chunk0 + ch, hd * V_ROWS:(hd + 1) * V_ROWS, :]
            o_t = o_t + jnp.dot(v_t, p, preferred_element_type=F32)
        denom = o_t[NA_HEAD_DIM:NA_HEAD_DIM + 1]
        o_ref[0, hd * NA_HEAD_DIM:(hd + 1) * NA_HEAD_DIM, blk * NA_Q_TILE:(blk + 1) * NA_Q_TILE] = (
            o_t[:NA_HEAD_DIM] * (1.0 / denom)).astype(BF16)

    wins = [window(NA_BLOCKS_PER_STEP * i + blk) for blk in range(NA_BLOCKS_PER_STEP)]
    win_next = window(jnp.minimum(NA_BLOCKS_PER_STEP * (i + 1), nblk - 1))

    @pl.when(i == 0)
    def _first_step_of_batch():
        for n in range(lookahead):
            m_ref[n:n + 1, :] = scores(q_ref, wins[items[n][0]], items[n], n % nbuf)

    maxima = {n: m_ref[n:n + 1, :] for n in range(lookahead)}
    for n, item in enumerate(items):
        ahead = n + lookahead
        if ahead < len(items):
            maxima[ahead] = scores(q_ref, wins[items[ahead][0]], items[ahead], ahead % nbuf)
        else:
            nxt = ahead - len(items)
            m_ref[nxt:nxt + 1, :] = scores(qnext_ref, win_next, items[nxt], nxt % nbuf)
        weighted_values(wins[item[0]], item, n % nbuf, maxima[n])


def _na_attention(q_t, k, v_t, bias, layer):
    batch, seq, _ = k.shape
    grid_rows = seq // GRID_W
    nblk = grid_rows // NA_Q_ROWS
    nsteps = nblk // NA_BLOCKS_PER_STEP
    nbuf = NA_LOOKAHEAD + 1
    assert nblk % NA_BLOCKS_PER_STEP == 0 and NA_HEADS % nbuf == 0 and NA_LOOKAHEAD < NA_HEADS
    tile = NA_BLOCKS_PER_STEP * NA_Q_TILE
    q_block = (1, NA_HEADS, HEAD_PAD, tile)
    return pl.pallas_call(
        functools.partial(_na_kernel, grid_rows=grid_rows),
        grid=(batch, nsteps),
        in_specs=[
            pl.BlockSpec(q_block, lambda b, i: (b, 0, 0, i)),
            pl.BlockSpec(q_block, lambda b, i: (b, 0, 0, jnp.minimum(i + 1, nsteps - 1))),
            pl.BlockSpec((1, seq, NA_WIDTH), lambda b, i: (b, 0, 0)),
            pl.BlockSpec((1, seq // NA_Q_TILE, NA_HEADS * V_ROWS, NA_Q_TILE), lambda b, i: (b, 0, 0, 0)),
            _const_spec((None, 3, NA_HEADS, NA_KEY_TILE, NA_Q_TILE), lambda b, i: (layer, 0, 0, 0, 0)),
        ],
        out_specs=pl.BlockSpec((1, NA_WIDTH, tile), lambda b, i: (b, 0, i)),
        out_shape=jax.ShapeDtypeStruct((batch, NA_WIDTH, seq), BF16),
        scratch_shapes=[pltpu.VMEM((nbuf, NA_KEY_TILE, NA_Q_TILE), F32), pltpu.VMEM((NA_HEADS, NA_Q_TILE), F32)],
        compiler_params=pltpu.CompilerParams(
            dimension_semantics=("parallel", "arbitrary"), vmem_limit_bytes=VMEM_LIMIT_BYTES),
        name="na_attention",
    )(q_t, q_t, k, v_t, bias)


def _na_bias_tables(rpb, grid_rows):
    depth = rpb.shape[0]
    n_col_off = 2 * NA_WIN_COLS - 1
    kc = np.arange(GRID_W)[:, None]
    c = np.arange(GRID_W)[None, :]
    cs = np.clip(c - NA_WIN_COLS // 2, 0, GRID_W - NA_WIN_COLS)
    col_ok = (kc >= cs) & (kc < cs + NA_WIN_COLS)
    col_off = kc - c + (NA_WIN_COLS - 1)
    select = ((col_off[None] == np.arange(n_col_off)[:, None, None]) & col_ok[None])
    select = select.reshape(n_col_off, GRID_W * GRID_W).astype(np.float32)
    col_tiles = jnp.dot((rpb * LOG2E).reshape(-1, n_col_off), select, precision=lax.Precision.HIGHEST)
    col_tiles = col_tiles.reshape(depth, NA_HEADS, 2 * NA_WIN_ROWS - 1, GRID_W, GRID_W)
    col_tiles = jnp.where(col_ok, col_tiles, MASK_VALUE)
    return pl.pallas_call(
        functools.partial(_bias_kernel, grid_rows=grid_rows),
        grid=(depth, NA_HEADS),
        in_specs=[pl.BlockSpec((1, 1, 2 * NA_WIN_ROWS - 1, GRID_W, GRID_W), lambda l, h: (l, h, 0, 0, 0))],
        out_specs=pl.BlockSpec((1, 3, 1, NA_KEY_TILE, NA_Q_TILE), lambda l, h: (l, 0, h, 0, 0)),
        out_shape=jax.ShapeDtypeStruct((depth, 3, NA_HEADS, NA_KEY_TILE, NA_Q_TILE), F32),
        name="na_bias",
    )(col_tiles)


def _bias_kernel(col_ref, o_ref, *, grid_rows):
    nblk = grid_rows // NA_Q_ROWS
    masked = jnp.full((GRID_W, GRID_W), MASK_VALUE, F32)
    for pattern, i in enumerate((0, 1, nblk - 1)):
        key_row0 = min(max(NA_Q_ROWS * i - NA_WIN_ROWS // 2, 0), grid_rows - NA_KEY_ROWS)
        for kr_rel in range(NA_KEY_ROWS):
            kr = key_row0 + kr_rel
            tiles = []
            for j in range(NA_Q_ROWS):
                r = NA_Q_ROWS * i + j
                rs = min(max(r - NA_WIN_ROWS // 2, 0), grid_rows - NA_WIN_ROWS)
                inside = rs <= kr < rs + NA_WIN_ROWS
                tiles.append(col_ref[0, 0, kr - r + NA_WIN_ROWS - 1] if inside else masked)
            o_ref[0, pattern, 0, kr_rel * GRID_W:(kr_rel + 1) * GRID_W, :] = jnp.concatenate(tiles, axis=1)


def _mla_kernel(q_ref, qnext_ref, k_ref, v_ref, o_ref, s_ref, m_ref):
    tq = MLA_Q_TILE
    nk = v_ref.shape[2]
    tk = v_ref.shape[-1]
    sub = F32_SUBLANES
    nbuf = s_ref.shape[0]
    lookahead = nbuf - 1
    items = [(tile, hd) for tile in range(MLA_TILES_PER_STEP) for hd in range(MLA_HEADS)]

    def score_tile(src_ref, item, slot, j, m_sub):
        tile, hd = item
        q_t = src_ref[0, hd, :, tile * tq:(tile + 1) * tq]
        s = jnp.dot(k_ref[0, hd, j * tk:(j + 1) * tk, :], q_t, preferred_element_type=F32)
        s_ref[slot, j * tk:(j + 1) * tk, :] = s
        m_j = jnp.max(s.reshape(tk // sub, sub, tq), axis=0)
        return m_j if m_sub is None else jnp.maximum(m_sub, m_j)

    def value_tile(hd, slot, j, m, acc):
        p = jnp.exp2(s_ref[slot, j * tk:(j + 1) * tk, :] - m).astype(BF16)
        return acc + jnp.dot(v_ref[0, hd, j], p, preferred_element_type=F32)

    @pl.when(pl.program_id(1) == 0)
    def _first_step_of_batch():
        for n in range(lookahead):
            m_sub = None
            for j in range(nk):
                m_sub = score_tile(q_ref, items[n], n % nbuf, j, m_sub)
            m_ref[n:n + 1, :] = jnp.max(m_sub, axis=0, keepdims=True)

    maxima = {n: m_ref[n:n + 1, :] for n in range(lookahead)}
    for n, (tile, hd) in enumerate(items):
        ahead = n + lookahead
        src_ref, ahead_n = (q_ref, ahead) if ahead < len(items) else (qnext_ref, ahead - len(items))
        m_sub = None
        acc = jnp.zeros((V_ROWS, tq), F32)
        for j in range(nk):
            m_sub = score_tile(src_ref, items[ahead_n], ahead % nbuf, j, m_sub)
            acc = value_tile(hd, n % nbuf, j, maxima[n], acc)
        m_ahead = jnp.max(m_sub, axis=0, keepdims=True)
        if ahead < len(items):
            maxima[ahead] = m_ahead
        else:
            m_ref[ahead_n:ahead_n + 1, :] = m_ahead
        o_ref[0, hd * MLA_V_DIM:(hd + 1) * MLA_V_DIM, tile * tq:(tile + 1) * tq] = (
            acc[:MLA_V_DIM] * (1.0 / acc[MLA_V_DIM:MLA_V_DIM + 1])).astype(BF16)


def _mla_attention(q_t, k, v_t):
    batch, _, seq, _ = k.shape
    nk, tk = v_t.shape[2], v_t.shape[4]
    step_q = MLA_TILES_PER_STEP * MLA_Q_TILE
    nsteps = seq // step_q
    nbuf = MLA_LOOKAHEAD + 1
    assert seq % step_q == 0 and MLA_HEADS % nbuf == 0 and MLA_LOOKAHEAD < MLA_HEADS
    q_block = (1, MLA_HEADS, HEAD_PAD, step_q)
    return pl.pallas_call(
        _mla_kernel,
        grid=(batch, nsteps),
        in_specs=[
            pl.BlockSpec(q_block, lambda b, i: (b, 0, 0, i)),
            pl.BlockSpec(q_block, lambda b, i: (b, 0, 0, jnp.minimum(i + 1, nsteps - 1))),
            pl.BlockSpec((1, MLA_HEADS, seq, HEAD_PAD), lambda b, i: (b, 0, 0, 0)),
            pl.BlockSpec((1, MLA_HEADS, nk, V_ROWS, tk), lambda b, i: (b, 0, 0, 0, 0)),
        ],
        out_specs=pl.BlockSpec((1, MLA_WIDTH, step_q), lambda b, i: (b, 0, i)),
        out_shape=jax.ShapeDtypeStruct((batch, MLA_WIDTH, seq), BF16),
        scratch_shapes=[pltpu.VMEM((nbuf, seq, MLA_Q_TILE), F32), pltpu.VMEM((MLA_HEADS, MLA_Q_TILE), F32)],
        compiler_params=pltpu.CompilerParams(
            dimension_semantics=("parallel", "arbitrary"), vmem_limit_bytes=VMEM_LIMIT_BYTES),
        name="mla_attention",
    )(q_t, q_t, k, v_t)


def _post_kernel(x_ref, ona_ref, omla_ref, gates_ref, mod_ref, wbn_ref, wbm_ref, wout_ref, gmlp_ref,
                 w1_ref, w2_ref, gfin_ref, o_ref, *, final):
    tm, d = x_ref.shape[1], x_ref.shape[2]
    d_ff = w1_ref.shape[-1]
    mod = mod_ref[0]
    gate_a, shift_m, scale_m, gate_m = mod[2:3], mod[3:4], mod[4:5], mod[5:6]

    def mix_tokens(rows):
        gates = gates_ref[0, rows, :]
        br_na = _tn_dot(ona_ref[0, :, rows], wbn_ref[...])
        br_mla = _tn_dot(omla_ref[0, :, rows], wbm_ref[...])
        merged = gates[:, :d].astype(F32) * br_na + gates[:, d:].astype(F32) * br_mla
        x = x_ref[0, rows, :] + gate_a * jnp.dot(merged.astype(BF16), wout_ref[...], preferred_element_type=F32)
        return x, _modulated_norm(x, gmlp_ref[...], shift_m, scale_m)

    def mix_channels(x, h):
        y = jnp.zeros_like(x)
        for c0 in range(0, d_ff, POST_FF_CHUNK):
            u = jnp.dot(h, w1_ref[:, c0:c0 + POST_FF_CHUNK], preferred_element_type=F32)
            u = jnp.square(jnp.maximum(u, 0.0)).astype(BF16)
            y = y + jnp.dot(u, w2_ref[c0:c0 + POST_FF_CHUNK, :], preferred_element_type=F32)
        x = x + gate_m * y
        return _rms(x, gfin_ref[...]) if final else x

    rows = [pl.ds(r0, tm // POST_SPLIT) for r0 in range(0, tm, tm // POST_SPLIT)]
    mixed = [mix_tokens(r) for r in rows]
    for r, (x, h) in zip(rows, mixed):
        o_ref[0, r, :] = mix_channels(x, h)


def _post(x, o_na, o_mla, gates, mod_l, layer, p, g_final, final):
    batch, seq, d = x.shape
    d_ff = p["w1"].shape[-1]
    tm = POST_TILE
    tok = lambda b, t: (b, t, 0)
    per_batch = lambda b, t: (b, 0, 0)
    w3 = lambda b, t: (layer, 0, 0)
    return pl.pallas_call(
        functools.partial(_post_kernel, final=final),
        grid=(batch, seq // tm),
        in_specs=[
            pl.BlockSpec((1, tm, d), tok),
            pl.BlockSpec((1, NA_WIDTH, tm), lambda b, t: (b, 0, t)),
            pl.BlockSpec((1, MLA_WIDTH, tm), lambda b, t: (b, 0, t)),
            pl.BlockSpec((1, tm, 2 * d), tok),
            pl.BlockSpec((1, 6, d), per_batch),
            _const_spec((None, NA_WIDTH, d), w3),
            _const_spec((None, MLA_WIDTH, d), w3),
            _const_spec((None, d, d), w3),
            _const_spec((None, 1, d), w3),
            _const_spec((None, d, d_ff), w3),
            _const_spec((None, d_ff, d), w3),
            _const_spec((1, d), lambda b, t: (0, 0)),
        ],
        out_specs=pl.BlockSpec((1, tm, d), tok),
        out_shape=jax.ShapeDtypeStruct((batch, seq, d), F32),
        compiler_params=pltpu.CompilerParams(
            dimension_semantics=("parallel", "parallel"), vmem_limit_bytes=VMEM_LIMIT_BYTES),
        name="post",
    )(x, o_na, o_mla, gates, mod_l, p["w_br_na"], p["w_br_mla"], p["w_out"], p["g_mlp"], p["w1"], p["w2"],
      g_final)


def _prepare_params(g_mix, w_in, b_gate, g_q, w_uq, g_kv, w_ukv, w_br_na, w_br_mla, w_out, g_mlp, w_ff1, w_ff2):
    depth, d, _ = w_in.shape
    o = 0
    cols = {}
    for name, size in (("q_na", NA_WIDTH), ("k_na", NA_WIDTH), ("v_na", NA_WIDTH), ("c_q", MLA_Q_RANK),
                       ("c_kv", MLA_KV_RANK), ("k_rope", MLA_ROPE_DIM), ("gate", 2 * d)):
        cols[name] = w_in[:, :, o:o + size]
        o += size
    half = MLA_ROPE_DIM // 2
    rope_pad = jnp.zeros((depth, d, LANES - MLA_ROPE_DIM), F32)
    kr = cols["k_rope"]
    kr_swapped = jnp.concatenate([kr[:, :, half:], kr[:, :, :half]], axis=-1)
    wc = jnp.concatenate([cols["c_q"], cols["c_kv"], kr, rope_pad, kr_swapped, rope_pad], axis=-1)
    wuq_t = w_uq.transpose(0, 2, 1)

    wukv = w_ukv.reshape(depth, MLA_KV_RANK, MLA_HEADS, MLA_NOPE_DIM + MLA_V_DIM)
    wuk = jnp.pad(wukv[..., :MLA_NOPE_DIM], ((0, 0), (0, 0), (0, 0), (0, HEAD_PAD - MLA_NOPE_DIM)))
    wuk = wuk.reshape(depth, MLA_KV_RANK, MLA_HEADS * HEAD_PAD)
    place = jnp.zeros((LANES, MLA_HEADS, HEAD_PAD), F32)
    j = jnp.arange(MLA_ROPE_DIM)
    place = place.at[j, :, MLA_NOPE_DIM + j].set(1.0).reshape(LANES, MLA_HEADS * HEAD_PAD)
    wkk = jnp.concatenate([wuk, jnp.broadcast_to(place, (depth,) + place.shape)], axis=1)
    wuv_t = wukv[..., MLA_NOPE_DIM:].reshape(depth, MLA_KV_RANK, MLA_WIDTH).transpose(0, 2, 1)

    return {
        "g_mix": g_mix.reshape(depth, 1, d),
        "wqv_t": jnp.concatenate([cols["q_na"], cols["v_na"]], axis=-1).transpose(0, 2, 1).astype(BF16),
        "wk": cols["k_na"].astype(BF16),
        "wc": wc.astype(BF16),
        "wg": cols["gate"].astype(BF16),
        "b_gate": b_gate.reshape(depth, 1, 2 * d),
        "g_q": g_q.reshape(depth, 1, MLA_Q_RANK),
        "g_kv": g_kv.reshape(depth, 1, MLA_KV_RANK),
        "wuq_t": wuq_t.astype(BF16),
        "wkk": wkk.astype(BF16),
        "wuv_t": wuv_t.astype(BF16),
        "w_br_na": w_br_na.astype(BF16),
        "w_br_mla": w_br_mla.astype(BF16),
        "w_out": w_out.astype(BF16),
        "g_mlp": g_mlp.reshape(depth, 1, d),
        "w1": w_ff1.astype(BF16),
        "w2": w_ff2.astype(BF16),
    }


def _rope_tables(positions):
    inv_freq = 1.0 / (ROPE_THETA ** (jnp.arange(0, MLA_ROPE_DIM, 2, dtype=F32) / MLA_ROPE_DIM))
    ang_t = positions.astype(F32)[:, None, :] * inv_freq[None, :, None]
    return jnp.cos(ang_t), jnp.sin(ang_t)


def kernel(x, c, positions, w_ada, b_ada, g_mix, w_in, b_gate, rpb, g_q, w_uq, g_kv, w_ukv, w_br_na, w_br_mla,
           w_out, g_mlp, w_ff1, w_ff2, g_final):
    batch, seq, d = x.shape
    depth = w_in.shape[0]
    grid_rows = seq // GRID_W
    assert seq % TOKEN_TILE == 0 and grid_rows % NA_Q_ROWS == 0 and grid_rows >= NA_KEY_ROWS + NA_Q_ROWS

    p = _prepare_params(g_mix, w_in, b_gate, g_q, w_uq, g_kv, w_ukv, w_br_na, w_br_mla, w_out, g_mlp,
                        w_ff1, w_ff2)
    rope = _rope_tables(positions)
    bias = _na_bias_tables(rpb, grid_rows)
    mod = _ada_mod(c, w_ada, b_ada).reshape(depth, batch, 6, d)
    g_fin = g_final.reshape(1, d)

    for layer in range(depth):
        q_na, k_na, v_na, q_m, k_m, v_m, gates = _projections(x, mod[layer], layer, p, rope)
        o_na = _na_attention(q_na, k_na, v_na, bias, layer)
        o_mla = _mla_attention(q_m, k_m, v_m)
        x = _post(x, o_na, o_mla, gates, mod[layer], layer, p, g_fin, layer == depth - 1)
    return x
```
